```python
import math
import jax, jax.numpy as jnp
from jax import lax
import numpy as np

D_MODEL = 2048
BATCH = 4
SEQ = 4096
DEPTH = 1

GRID_W = 64
CTX_LEN = 256
N_DIFF_HEADS = 8
DIFF_HEAD_DIM = 64
DIFF_V_DIM = 2 * DIFF_HEAD_DIM
ATTN_WIDTH = N_DIFF_HEADS * DIFF_V_DIM
CONV_WIDTH = D_MODEL - ATTN_WIDTH
CONV_GROUPS = 8
CONV_KERNEL = 31
IN_COLS = 3 * ATTN_WIDTH + 2 * CONV_WIDTH
D_FF = 4 * D_MODEL
ROPE_BASE = 10000.0
Q_BLOCK = 128
LN_EPS = 1e-5
DEEPNORM_ALPHA = (2.0 * DEPTH) ** 0.25
DEEPNORM_BETA = (8.0 * DEPTH) ** -0.25

kernel_name = "hymba_style_diffattn_conformer_dit_block"


def layer_norm(x, g=None, b=None):
    xf = x.astype(jnp.float32)
    mu = jnp.mean(xf, axis=-1, keepdims=True)
    var = jnp.mean(jnp.square(xf - mu), axis=-1, keepdims=True)
    y = (xf - mu) * lax.rsqrt(var + LN_EPS)
    if g is not None:
        y = y * g.astype(jnp.float32) + b.astype(jnp.float32)
    return y.astype(x.dtype)


def rms_norm(x, g):
    xf = x.astype(jnp.float32)
    y = xf * lax.rsqrt(jnp.mean(jnp.square(xf), axis=-1, keepdims=True) + LN_EPS)
    return (y * g.astype(jnp.float32)).astype(x.dtype)


def modulate(x, shift, scale):
    return layer_norm(x) * (1 + scale) + shift


def axial_rope_tables(n_tokens):
    rows = n_tokens // GRID_W
    row = jnp.repeat(jnp.arange(rows), GRID_W, total_repeat_length=n_tokens).astype(jnp.float32)
    col = jnp.tile(jnp.arange(GRID_W), rows).astype(jnp.float32)
    axis_dim = DIFF_HEAD_DIM // 2
    inv_freq = ROPE_BASE ** (-jnp.arange(0, axis_dim, 2, dtype=jnp.float32) / axis_dim)
    ang_r = row[:, None] * inv_freq
    ang_c = col[:, None] * inv_freq
    ang = jnp.concatenate([ang_r, ang_r, ang_c, ang_c], axis=-1)
    return jnp.cos(ang), jnp.sin(ang)


def _rotate_half(u):
    u1, u2 = jnp.split(u, 2, axis=-1)
    return jnp.concatenate([-u2, u1], axis=-1)


def apply_axial_rope(x, cos, sin):
    xr, xc = jnp.split(x, 2, axis=-1)
    rotated = jnp.concatenate([_rotate_half(xr), _rotate_half(xc)], axis=-1)
    cos = cos[:, None, None, :].astype(x.dtype)
    sin = sin[:, None, None, :].astype(x.dtype)
    return x * cos + rotated * sin


def diff_attention(q, k, v, lam, lam_init, subln_g):
    B, Lq = q.shape[0], q.shape[1]
    nb = Lq // Q_BLOCK
    qb = q.reshape(B, nb, Q_BLOCK, N_DIFF_HEADS, 2, DIFF_HEAD_DIM).transpose(1, 0, 2, 3, 4, 5)
    scale = DIFF_HEAD_DIM ** -0.5

    def block(qi):
        s = jnp.einsum('bqhcd,bkhcd->bhcqk', qi, k, preferred_element_type=jnp.float32) * scale
        p = jax.nn.softmax(s, axis=-1)
        a = p[:, :, 0] - lam * p[:, :, 1]
        return jnp.einsum('bhqk,bkhe->bqhe', a.astype(v.dtype), v)

    o = lax.map(block, qb)
    o = o.transpose(1, 0, 2, 3, 4).reshape(B, Lq, N_DIFF_HEADS, DIFF_V_DIM)
    o = rms_norm(o, subln_g) * (1.0 - lam_init)
    return o.reshape(B, Lq, ATTN_WIDTH)


def conformer_conv(u, w_dw, b_dw, g, b):
    a, gate = jnp.split(u, 2, axis=-1)
    h = a * jax.nn.sigmoid(gate)
    h = lax.conv_general_dilated(
        h, w_dw[:, None, :].astype(h.dtype), window_strides=(1,),
        padding=[(CONV_KERNEL // 2, CONV_KERNEL // 2)],
        dimension_numbers=('NWC', 'WIO', 'NWC'),
        feature_group_count=CONV_WIDTH) + b_dw
    return jax.nn.silu(layer_norm(h, g, b))


def sq_relu_mlp(h, w1, b1, w2, b2):
    return jnp.square(jax.nn.relu(h @ w1 + b1)) @ w2 + b2


def hybrid_layer(x, ctx, c, c_ctx, cos, sin, lam_init, update_ctx,
                 w_ada, b_ada, w_in, b_glu, lq1, lk1, lq2, lk2, subln_g,
                 w_dw, b_dw, conv_ln_g, conv_ln_b, w_out, b_out, ln1_g, ln1_b,
                 w_ff1, b_ff1, w_ff2, b_ff2, ln2_g, ln2_b):
    B, L, _ = x.shape
    C = ctx.shape[1]
    A = ATTN_WIDTH
    ada = (jax.nn.silu(c) @ w_ada + b_ada)[:, None, :]
    sh1, sc1, g1, sh2, sc2, g2 = jnp.split(ada, 6, axis=-1)
    ada_c = jax.nn.silu(c_ctx) @ w_ada + b_ada
    csh1, csc1, cg1, csh2, csc2, cg2 = jnp.split(ada_c, 6, axis=-1)

    lam = (jnp.exp(jnp.sum(lq1.astype(jnp.float32) * lk1.astype(jnp.float32)))
           - jnp.exp(jnp.sum(lq2.astype(jnp.float32) * lk2.astype(jnp.float32))) + lam_init)

    h = modulate(x, sh1, sc1)
    hc = modulate(ctx, csh1, csc1)
    p = h @ w_in
    q, k, v, u = jnp.split(p, [A, 2 * A, 3 * A], axis=-1)
    q = apply_axial_rope(q.reshape(B, L, N_DIFF_HEADS, 2, DIFF_HEAD_DIM), cos, sin)
    k = apply_axial_rope(k.reshape(B, L, N_DIFF_HEADS, 2, DIFF_HEAD_DIM), cos, sin)
    v = v.reshape(B, L, N_DIFF_HEADS, DIFF_V_DIM)
    kvc = hc @ w_in[:, A:3 * A]
    kc, vc = jnp.split(kvc, 2, axis=-1)
    kc = kc.reshape(B, C, N_DIFF_HEADS, 2, DIFF_HEAD_DIM)
    vc = vc.reshape(B, C, N_DIFF_HEADS, DIFF_V_DIM)

    k_all = jnp.concatenate([k, kc], axis=1)
    v_all = jnp.concatenate([v, vc], axis=1)
    attn = diff_attention(q, k_all, v_all, lam, lam_init, subln_g)
    conv = conformer_conv(u + b_glu, w_dw, b_dw, conv_ln_g, conv_ln_b)
    y = jnp.concatenate([attn, conv], axis=-1) @ w_out + b_out
    x = layer_norm(DEEPNORM_ALPHA * x + g1 * y, ln1_g, ln1_b)

    h2 = modulate(x, sh2, sc2)
    x = layer_norm(DEEPNORM_ALPHA * x + g2 * sq_relu_mlp(h2, w_ff1, b_ff1, w_ff2, b_ff2), ln2_g, ln2_b)

    if update_ctx:
        qc = (hc @ w_in[:, :A]).reshape(B, C, N_DIFF_HEADS, 2, DIFF_HEAD_DIM)
        uc = hc @ w_in[:, 3 * A:] + b_glu
        attn_c = diff_attention(qc, kc, vc, lam, lam_init, subln_g)
        conv_c = conformer_conv(uc, w_dw, b_dw, conv_ln_g, conv_ln_b)
        yc = jnp.concatenate([attn_c, conv_c], axis=-1) @ w_out + b_out
        ctx = layer_norm(DEEPNORM_ALPHA * ctx + cg1 * yc, ln1_g, ln1_b)
        hc2 = modulate(ctx, csh2, csc2)
        ctx = layer_norm(DEEPNORM_ALPHA * ctx + cg2 * sq_relu_mlp(hc2, w_ff1, b_ff1, w_ff2, b_ff2),
                         ln2_g, ln2_b)
    return x, ctx


def setup_inputs(seed: int = 0) -> dict:
    key = jax.random.key(seed)
    ks = jax.random.split(key, 32)
    f32 = jnp.float32
    D = D_MODEL

    def nrm(k, shape, std):
        return jax.random.normal(k, shape, f32) * std

    return {
        "x": nrm(ks[0], (BATCH, SEQ, D), 1.0),
        "c": nrm(ks[1], (BATCH, D), 1.0),
        "ctx": nrm(ks[2], (BATCH, CTX_LEN, D), 1.0),
        "c_ctx": nrm(ks[3], (D,), 1.0),
        "w_ada": nrm(ks[4], (DEPTH, D, 6 * D), D ** -0.5),
        "b_ada": nrm(ks[5], (DEPTH, 6 * D), 0.01),
        "w_in": nrm(ks[6], (DEPTH, D, IN_COLS), D ** -0.5),
        "b_glu": nrm(ks[7], (DEPTH, 2 * CONV_WIDTH), 0.01),
        "lambda_q1": nrm(ks[8], (DEPTH, DIFF_HEAD_DIM), 0.1),
        "lambda_k1": nrm(ks[9], (DEPTH, DIFF_HEAD_DIM), 0.1),
        "lambda_q2": nrm(ks[10], (DEPTH, DIFF_HEAD_DIM), 0.1),
        "lambda_k2": nrm(ks[11], (DEPTH, DIFF_HEAD_DIM), 0.1),
        "subln_g": 1.0 + nrm(ks[12], (DEPTH, DIFF_V_DIM), 0.01),
        "w_dw": nrm(ks[13], (DEPTH, CONV_KERNEL, CONV_WIDTH), CONV_KERNEL ** -0.5),
        "b_dw": nrm(ks[14], (DEPTH, CONV_WIDTH), 0.01),
        "conv_ln_g": 1.0 + nrm(ks[15], (DEPTH, CONV_WIDTH), 0.01),
        "conv_ln_b": nrm(ks[16], (DEPTH, CONV_WIDTH), 0.01),
        "w_out": nrm(ks[17], (DEPTH, D, D), D ** -0.5 * DEEPNORM_BETA),
        "b_out": nrm(ks[18], (DEPTH, D), 0.01),
        "ln1_g": 1.0 + nrm(ks[19], (DEPTH, D), 0.01),
        "ln1_b": nrm(ks[20], (DEPTH, D), 0.01),
        "w_ff1": nrm(ks[21], (DEPTH, D, D_FF), D ** -0.5),
        "b_ff1": nrm(ks[22], (DEPTH, D_FF), 0.01),
        "w_ff2": nrm(ks[23], (DEPTH, D_FF, D), D_FF ** -0.5 * DEEPNORM_BETA),
        "b_ff2": nrm(ks[24], (DEPTH, D), 0.01),
        "ln2_g": 1.0 + nrm(ks[25], (DEPTH, D), 0.01),
        "ln2_b": nrm(ks[26], (DEPTH, D), 0.01),
    }


def reference(x, c, ctx, c_ctx, w_ada, b_ada, w_in, b_glu, lambda_q1, lambda_k1,
              lambda_q2, lambda_k2, subln_g, w_dw, b_dw, conv_ln_g, conv_ln_b,
              w_out, b_out, ln1_g, ln1_b, w_ff1, b_ff1, w_ff2, b_ff2, ln2_g, ln2_b):
    cos, sin = axial_rope_tables(x.shape[1])
    for l in range(DEPTH):
        lam_init = 0.8 - 0.6 * math.exp(-0.3 * l)
        x, ctx = hybrid_layer(
            x, ctx, c, c_ctx, cos, sin, lam_init, l < DEPTH - 1,
            w_ada[l], b_ada[l], w_in[l], b_glu[l], lambda_q1[l], lambda_k1[l],
            lambda_q2[l], lambda_k2[l], subln_g[l], w_dw[l], b_dw[l], conv_ln_g[l],
            conv_ln_b[l], w_out[l], b_out[l], ln1_g[l], ln1_b[l], w_ff1[l], b_ff1[l],
            w_ff2[l], b_ff2[l], ln2_g[l], ln2_b[l])
    return x
```

```python
import functools
import math

import jax
import jax.numpy as jnp
from jax import lax
from jax.experimental import pallas as pl
from jax.experimental.pallas import tpu as pltpu

F32 = jnp.float32
BF16 = jnp.bfloat16

GRID_W = 64
N_HEADS = 8
HEAD_DIM = 64
HEAD_W = 2 * HEAD_DIM
CONV_KERNEL = 31
CONV_PAD = CONV_KERNEL // 2
ROPE_BASE = 10000.0
LN_EPS = 1e-5
DEPTH = 1
DEEPNORM_ALPHA = (2.0 * DEPTH) ** 0.25
LAM_INIT = 0.8 - 0.6 * math.exp(-0.3 * 0)

V7X_VMEM_LIMIT = 60 * 1024 * 1024
HALO = 16


def _params(sem, vmem=V7X_VMEM_LIMIT):
    return pltpu.CompilerParams(dimension_semantics=sem, vmem_limit_bytes=vmem)


def _resident(shape):
    return pl.BlockSpec(shape, lambda *_: (0,) * len(shape), pipeline_mode=pl.Buffered(1))


def _normalize(x):
    mu = jnp.mean(x, axis=-1, keepdims=True)
    xc = x - mu
    var = jnp.mean(xc * xc, axis=-1, keepdims=True)
    return xc * lax.rsqrt(var + LN_EPS)


def _sigmoid(x):
    return 1.0 / (1.0 + jnp.exp(-x))


def _ada_kernel(c_ref, w_ref, b_ref, o_ref):
    c = c_ref[...]
    s = (c * _sigmoid(c)).astype(BF16)
    o_ref[...] = jnp.dot(s, w_ref[...].astype(BF16), preferred_element_type=F32) + b_ref[...]


def _ada(c8, w_ada, b_ada, tn=1024):
    rows, d = c8.shape
    n = w_ada.shape[1]
    return pl.pallas_call(
        _ada_kernel,
        grid=(n // tn,),
        in_specs=[pl.BlockSpec((rows, d), lambda j: (0, 0)),
                  pl.BlockSpec((d, tn), lambda j: (0, j)),
                  pl.BlockSpec((1, tn), lambda j: (0, j))],
        out_specs=pl.BlockSpec((rows, tn), lambda j: (0, j)),
        out_shape=jax.ShapeDtypeStruct((rows, n), F32),
        compiler_params=_params(("arbitrary",)),
        name="ada",
    )(c8, w_ada, b_ada)


def _inproj_kernel(x_ref, mod_ref, w_ref, bglu_ref, cos_ref, sina_ref, sinb_ref,
                   q_ref, k_ref, v_ref, u_ref, *, attn_w, conv_w, chunk):
    mod = mod_ref[0]
    h = (_normalize(x_ref[...]) * (1.0 + mod[1:2]) + mod[0:1]).astype(BF16)
    cos, sina, sinb = cos_ref[...], sina_ref[...], sinb_ref[...]

    def proj(c0):
        return jnp.dot(h, w_ref[:, c0:c0 + chunk], preferred_element_type=F32)

    def rope(p):
        return p * cos + pltpu.roll(p, HEAD_W - 16, axis=1) * sina + pltpu.roll(p, 16, axis=1) * sinb

    for dst, base, scale in ((q_ref, 0, HEAD_DIM ** -0.5), (k_ref, attn_w, None)):
        for j in range(attn_w // chunk):
            p = proj(base + j * chunk)
            for hh in range(chunk // HEAD_W):
                r = rope(p[:, hh * HEAD_W:(hh + 1) * HEAD_W])
                if scale is not None:
                    r = r * scale
                c0 = j * chunk + hh * HEAD_W
                dst[:, c0:c0 + HEAD_W] = r.astype(BF16)
    for j in range(attn_w // chunk):
        v_ref[:, j * chunk:(j + 1) * chunk] = proj(2 * attn_w + j * chunk).astype(BF16)
    for j in range(conv_w // chunk):
        a = proj(3 * attn_w + j * chunk) + bglu_ref[:, j * chunk:(j + 1) * chunk]
        g = proj(3 * attn_w + conv_w + j * chunk) + bglu_ref[:, conv_w + j * chunk:conv_w + (j + 1) * chunk]
        u_ref[:, j * chunk:(j + 1) * chunk] = a * _sigmoid(g)


def _inproj(x2d, mod, w_in, b_glu, cos, sina, sinb, seq, attn_w, conv_w, tm=512, chunk=512):
    m, d = x2d.shape
    tiles_per_seq = seq // tm
    kern = functools.partial(_inproj_kernel, attn_w=attn_w, conv_w=conv_w, chunk=chunk)
    tab = pl.BlockSpec((tm, HEAD_W), lambda i: (i % tiles_per_seq, 0))
    row = lambda w: pl.BlockSpec((tm, w), lambda i: (i, 0))
    return pl.pallas_call(
        kern,
        grid=(m // tm,),
        in_specs=[row(d),
                  pl.BlockSpec((1,) + mod.shape[1:], lambda i: (i // tiles_per_seq, 0, 0)),
                  _resident(w_in.shape), _resident(b_glu.shape), tab, tab, tab],
        out_specs=[row(attn_w), row(attn_w), row(attn_w), row(conv_w)],
        out_shape=[jax.ShapeDtypeStruct((m, attn_w), BF16)] * 3 + [jax.ShapeDtypeStruct((m, conv_w), F32)],
        compiler_params=_params(("arbitrary",)),
        name="inproj",
    )(x2d, mod, w_in, b_glu, cos, sina, sinb)


def _inproj_ctx_kernel(x_ref, mod_ref, wk_ref, wv_ref, k_ref, v_ref, *, chunk):
    mod = mod_ref[0]
    h = (_normalize(x_ref[...]) * (1.0 + mod[1:2]) + mod[0:1]).astype(BF16)
    for w_ref, dst in ((wk_ref, k_ref), (wv_ref, v_ref)):
        for j in range(w_ref.shape[1] // chunk):
            sl = slice(j * chunk, (j + 1) * chunk)
            dst[:, sl] = jnp.dot(h, w_ref[:, sl], preferred_element_type=F32).astype(BF16)


def _inproj_ctx(ctx2d, mod, w_in, ctx_row, attn_w, tm=512, chunk=512):
    m, d = ctx2d.shape
    kern = functools.partial(_inproj_ctx_kernel, chunk=chunk)
    row = lambda w: pl.BlockSpec((tm, w), lambda i: (i, 0))
    return pl.pallas_call(
        kern,
        grid=(m // tm,),
        in_specs=[row(d),
                  pl.BlockSpec((1,) + mod.shape[1:], lambda i: (ctx_row, 0, 0)),
                  pl.BlockSpec((d, attn_w), lambda i: (0, 1)),
                  pl.BlockSpec((d, attn_w), lambda i: (0, 2))],
        out_specs=[row(attn_w), row(attn_w)],
        out_shape=[jax.ShapeDtypeStruct((m, attn_w), BF16)] * 2,
        compiler_params=_params(("arbitrary",)),
        name="inproj_ctx",
    )(ctx2d, mod, w_in, w_in)


def _attn_kernel(lamv_ref, g_ref, q_ref, k_ref, kc_ref, v_ref, vc_ref, o_ref, s_ref, *, tq, tk):
    seq = k_ref.shape[1]
    n_ctx = kc_ref.shape[1]
    lv = lamv_ref[...]
    lam = (jnp.exp(jnp.sum(lv[0:1] * lv[1:2], axis=-1, keepdims=True))
           - jnp.exp(jnp.sum(lv[2:3] * lv[3:4], axis=-1, keepdims=True)) + LAM_INIT)
    lane = lax.broadcasted_iota(jnp.int32, (tq, HEAD_W), 1)
    chunks = [(k_ref, v_ref, o, tk, o) for o in range(0, seq, tk)] + [(kc_ref, vc_ref, 0, n_ctx, seq)]
    nt = (((1,), (1,)), ((), ()))

    def tile(i, carry):
        r0 = pl.multiple_of(i * tq, tq)
        q = q_ref[0, pl.ds(r0, tq), :]
        outs = []
        for c in range(2):
            qc = jnp.where((lane >= c * HEAD_DIM) & (lane < (c + 1) * HEAD_DIM), q, jnp.zeros_like(q))
            mlane = jnp.full((tq, HEAD_W), -jnp.inf, F32)
            for kr, _, off, n, pos in chunks:
                s = lax.dot_general(qc, kr[0, off:off + n, :], nt, preferred_element_type=F32)
                s_ref[c, :, pos:pos + n] = s
                for t in range(n // HEAD_W):
                    mlane = jnp.maximum(mlane, s[:, t * HEAD_W:(t + 1) * HEAD_W])
            m = jnp.max(mlane, axis=-1, keepdims=True)
            llane = jnp.zeros((tq, HEAD_W), F32)
            acc = jnp.zeros((tq, HEAD_W), F32)
            for _, vr, off, n, pos in chunks:
                p = jnp.exp(s_ref[c, :, pos:pos + n] - m)
                for t in range(n // HEAD_W):
                    llane = llane + p[:, t * HEAD_W:(t + 1) * HEAD_W]
                acc = acc + jnp.dot(p.astype(BF16), vr[0, off:off + n, :], preferred_element_type=F32)
            outs.append(acc / jnp.sum(llane, axis=-1, keepdims=True))
        o = outs[0] - lam * outs[1]
        o = o * lax.rsqrt(jnp.mean(o * o, axis=-1, keepdims=True) + LN_EPS) * g_ref[...] * (1.0 - LAM_INIT)
        o_ref[0, pl.ds(r0, tq), :] = o.astype(BF16)
        return carry

    lax.fori_loop(0, seq // tq, tile, 0)


def _attention(lamv, subln_g, q, k, kc, v, vc, tq=256, tk=512):
    b, seq, attn_w = q.shape
    n_ctx = kc.shape[1]
    kern = functools.partial(_attn_kernel, tq=tq, tk=tk)
    head = lambda n: pl.BlockSpec((1, n, HEAD_W), lambda bi, hi: (bi, 0, hi))
    return pl.pallas_call(
        kern,
        grid=(b, attn_w // HEAD_W),
        in_specs=[pl.BlockSpec(lamv.shape, lambda bi, hi: (0, 0)),
                  pl.BlockSpec(subln_g.shape, lambda bi, hi: (0, 0)),
                  head(seq), head(seq), head(n_ctx), head(seq), head(n_ctx)],
        out_specs=head(seq),
        out_shape=jax.ShapeDtypeStruct((b, seq, attn_w), BF16),
        scratch_shapes=[pltpu.VMEM((2, tq, seq + n_ctx), F32)],
        compiler_params=_params(("arbitrary", "arbitrary")),
        name="attn",
    )(lamv, subln_g, q, k, kc, v, vc)


def _conv_kernel(u_ref, prev_ref, next_ref, w_ref, bdw_ref, g_ref, b_ref, o_ref, buf_ref, acc_ref,
                 *, tl, rows, lanes):
    i = pl.program_id(1)
    last = pl.num_programs(1) - 1
    width = u_ref.shape[2]
    buf_ref[HALO:HALO + tl, :] = u_ref[0]
    buf_ref[0:HALO, :] = jnp.where(i > 0, prev_ref[0], 0.0)
    buf_ref[HALO + tl:, :] = jnp.where(i < last, next_ref[0], 0.0)
    span = rows + 8 * ((CONV_KERNEL - 1) // 8)
    for r0 in range(0, tl, rows):
        for c0 in range(0, width, lanes):
            acc = jnp.zeros((rows, lanes), F32)
            for sub in range(8):
                start = r0 + HALO - CONV_PAD + sub
                xs = buf_ref[start:start + span, c0:c0 + lanes]
                for t in range(sub, CONV_KERNEL, 8):
                    acc = acc + w_ref[t:t + 1, c0:c0 + lanes] * xs[t - sub:t - sub + rows]
            acc_ref[r0:r0 + rows, c0:c0 + lanes] = acc + bdw_ref[:, c0:c0 + lanes]
    y = _normalize(acc_ref[...]) * g_ref[...] + b_ref[...]
    o_ref[0] = (y * _sigmoid(y)).astype(BF16)


def _conv(u, w_dw, b_dw, g, b, tl=512, rows=64, lanes=128):
    bsz, seq, width = u.shape
    halo_per_tile = tl // HALO
    n_halo = seq // HALO
    kern = functools.partial(_conv_kernel, tl=tl, rows=rows, lanes=lanes)
    vec = pl.BlockSpec((1, width), lambda bi, i: (0, 0))
    return pl.pallas_call(
        kern,
        grid=(bsz, seq // tl),
        in_specs=[pl.BlockSpec((1, tl, width), lambda bi, i: (bi, i, 0)),
                  pl.BlockSpec((1, HALO, width), lambda bi, i: (bi, jnp.maximum(i * halo_per_tile - 1, 0), 0)),
                  pl.BlockSpec((1, HALO, width),
                               lambda bi, i: (bi, jnp.minimum((i + 1) * halo_per_tile, n_halo - 1), 0)),
                  pl.BlockSpec(w_dw.shape, lambda bi, i: (0, 0)), vec, vec, vec],
        out_specs=pl.BlockSpec((1, tl, width), lambda bi, i: (bi, i, 0)),
        out_shape=jax.ShapeDtypeStruct((bsz, seq, width), BF16),
        scratch_shapes=[pltpu.VMEM((tl + 2 * HALO, width), F32), pltpu.VMEM((tl, width), F32)],
        compiler_params=_params(("arbitrary", "arbitrary")),
        name="conv",
    )(u, u, u, w_dw, b_dw, g, b)


def _outproj_kernel(a_ref, c_ref, x_ref, mod_ref, w_ref, bo_ref, g_ref, b_ref, o_ref, z_ref, *, chunk):
    attn_w = a_ref.shape[1]
    a = a_ref[...]
    cv = c_ref[...]
    gate = mod_ref[0, 2:3, :]
    for c0 in range(0, w_ref.shape[1], chunk):
        sl = slice(c0, c0 + chunk)
        y = (jnp.dot(a, w_ref[0:attn_w, sl], preferred_element_type=F32)
             + jnp.dot(cv, w_ref[attn_w:, sl], preferred_element_type=F32) + bo_ref[:, sl])
        z_ref[:, sl] = DEEPNORM_ALPHA * x_ref[:, sl] + gate[:, sl] * y
    o_ref[...] = _normalize(z_ref[...]) * g_ref[...] + b_ref[...]


def _outproj(attn2d, conv2d, x2d, mod, w_out, b_out, g, b, seq, tm=512, chunk=512):
    m, d = x2d.shape
    tiles_per_seq = seq // tm
    kern = functools.partial(_outproj_kernel, chunk=chunk)
    row = lambda w: pl.BlockSpec((tm, w), lambda i: (i, 0))
    vec = pl.BlockSpec((1, d), lambda i: (0, 0))
    return pl.pallas_call(
        kern,
        grid=(m // tm,),
        in_specs=[row(attn2d.shape[1]), row(conv2d.shape[1]), row(d),
                  pl.BlockSpec((1,) + mod.shape[1:], lambda i: (i // tiles_per_seq, 0, 0)),
                  _resident(w_out.shape), vec, vec, vec],
        out_specs=row(d),
        out_shape=jax.ShapeDtypeStruct((m, d), F32),
        scratch_shapes=[pltpu.VMEM((tm, d), F32)],
        compiler_params=_params(("arbitrary",)),
        name="outproj",
    )(attn2d, conv2d, x2d, mod, w_out, b_out, g, b)


def _ffn_kernel(x_ref, mod_ref, w1_ref, b1_ref, w2_ref, b2_ref, g_ref, b_ref, o_ref, h_ref):
    f = pl.program_id(1)
    mod = mod_ref[0]

    @pl.when(f == 0)
    def _():
        h_ref[...] = (_normalize(x_ref[...]) * (1.0 + mod[4:5]) + mod[3:4]).astype(BF16)
        o_ref[...] = jnp.zeros_like(o_ref)

    t = jnp.dot(h_ref[...], w1_ref[...], preferred_element_type=F32) + b1_ref[...]
    t = jnp.maximum(t, 0.0)
    o_ref[...] += jnp.dot((t * t).astype(BF16), w2_ref[...], preferred_element_type=F32)

    @pl.when(f == pl.num_programs(1) - 1)
    def _():
        z = DEEPNORM_ALPHA * x_ref[...] + mod[5:6] * (o_ref[...] + b2_ref[...])
        o_ref[...] = _normalize(z) * g_ref[...] + b_ref[...]


def _ffn(x2d, mod, w1, b1, w2, b2, g, b, seq, tm=1024, tf=512):
    m, d = x2d.shape
    dff = w1.shape[1]
    tiles_per_seq = seq // tm
    vec = pl.BlockSpec((1, d), lambda i, f: (0, 0))
    return pl.pallas_call(
        _ffn_kernel,
        grid=(m // tm, dff // tf),
        in_specs=[pl.BlockSpec((tm, d), lambda i, f: (i, 0)),
                  pl.BlockSpec((1,) + mod.shape[1:], lambda i, f: (i // tiles_per_seq, 0, 0)),
                  pl.BlockSpec((d, tf), lambda i, f: (0, f)),
                  pl.BlockSpec((1, tf), lambda i, f: (0, f)),
                  pl.BlockSpec((tf, d), lambda i, f: (f, 0)),
                  vec, vec, vec],
        out_specs=pl.BlockSpec((tm, d), lambda i, f: (i, 0)),
        out_shape=jax.ShapeDtypeStruct((m, d), F32),
        scratch_shapes=[pltpu.VMEM((tm, d), BF16)],
        compiler_params=_params(("arbitrary", "arbitrary")),
        name="ffn",
    )(x2d, mod, w1, b1, w2, b2, g, b)


def _rope_tables(seq):
    rows = seq // GRID_W
    row = jnp.repeat(jnp.arange(rows), GRID_W, total_repeat_length=seq).astype(F32)
    col = jnp.tile(jnp.arange(GRID_W), rows).astype(F32)
    axis_dim = HEAD_DIM // 2
    inv_freq = ROPE_BASE ** (-jnp.arange(0, axis_dim, 2, dtype=F32) / axis_dim)
    ang_r = row[:, None] * inv_freq
    ang_c = col[:, None] * inv_freq
    ang = jnp.concatenate([ang_r, ang_r, ang_c, ang_c] * 2, axis=-1)
    cos, sin = jnp.cos(ang), jnp.sin(ang)
    first = (jnp.arange(HEAD_W) % (2 * 16)) < 16
    return cos, jnp.where(first, -sin, 0.0), jnp.where(first, 0.0, sin)


def kernel(x, c, ctx, c_ctx, w_ada, b_ada, w_in, b_glu, lambda_q1, lambda_k1, lambda_q2, lambda_k2, subln_g, w_dw, b_dw, conv_ln_g, conv_ln_b, w_out, b_out, ln1_g, ln1_b, w_ff1, b_ff1, w_ff2, b_ff2, ln2_g, ln2_b):
    assert w_ada.shape[0] == DEPTH
    bsz, seq, d = x.shape
    n_ctx = ctx.shape[1]
    attn_w = N_HEADS * HEAD_W
    conv_w = d - attn_w
    assert w_in.shape[2] == 3 * attn_w + 2 * conv_w and seq % GRID_W == 0

    ctx_row = bsz
    c8 = jnp.concatenate([c, c_ctx[None, :], jnp.zeros((8 - bsz - 1, d), F32)], axis=0)
    mod = _ada(c8, w_ada[0], b_ada).reshape(8, 6, d)

    w_in_b = w_in[0].astype(BF16)
    cos, sina, sinb = _rope_tables(seq)
    x2d = x.reshape(bsz * seq, d)
    q, k, v, u = _inproj(x2d, mod, w_in_b, b_glu, cos, sina, sinb, seq, attn_w, conv_w)
    kc, vc = _inproj_ctx(ctx.reshape(bsz * n_ctx, d), mod, w_in_b, ctx_row, attn_w)

    lamv = jnp.concatenate([lambda_q1, lambda_k1, lambda_q2, lambda_k2], axis=0)
    attn = _attention(lamv, subln_g, q.reshape(bsz, seq, attn_w), k.reshape(bsz, seq, attn_w),
                      kc.reshape(bsz, n_ctx, attn_w), v.reshape(bsz, seq, attn_w),
                      vc.reshape(bsz, n_ctx, attn_w))

    w_dw_p = jnp.concatenate([w_dw[0], jnp.zeros((1, conv_w), F32)], axis=0)
    conv = _conv(u.reshape(bsz, seq, conv_w), w_dw_p, b_dw, conv_ln_g, conv_ln_b)

    x1 = _outproj(attn.reshape(bsz * seq, attn_w), conv.reshape(bsz * seq, conv_w), x2d, mod,
                  w_out[0].astype(BF16), b_out, ln1_g, ln1_b, seq)
    out = _ffn(x1, mod, w_ff1[0].astype(BF16), b_ff1, w_ff2[0].astype(BF16), b_ff2, ln2_g, ln2_b, seq)
    return out.reshape(bsz, seq, d)
```

```python
import functools
import math

import jax
import jax.numpy as jnp
from jax import lax
from jax.experimental import pallas as pl
from jax.experimental.pallas import tpu as pltpu

F32 = jnp.float32
BF16 = jnp.bfloat16

GRID_W = 64
N_HEADS = 8
HEAD_DIM = 64
HEAD_W = 2 * HEAD_DIM
CONV_KERNEL = 31
CONV_PAD = CONV_KERNEL // 2
ROPE_BASE = 10000.0
LN_EPS = 1e-5
DEPTH = 1
DEEPNORM_ALPHA = (2.0 * DEPTH) ** 0.25
LAM_INIT = 0.8 - 0.6 * math.exp(-0.3 * 0)
Q_SCALE = HEAD_DIM ** -0.5 * math.log2(math.e)

V7X_VMEM_LIMIT = 60 * 1024 * 1024
HALO = 16


def _params(sem, vmem=V7X_VMEM_LIMIT):
    return pltpu.CompilerParams(dimension_semantics=sem, vmem_limit_bytes=vmem)


def _resident(shape):
    return pl.BlockSpec(shape, lambda *_: (0,) * len(shape), pipeline_mode=pl.Buffered(1))


def _normalize(x):
    mu = jnp.mean(x, axis=-1, keepdims=True)
    xc = x - mu
    var = jnp.mean(xc * xc, axis=-1, keepdims=True)
    return xc * lax.rsqrt(var + LN_EPS)


def _sigmoid(x):
    return 1.0 / (1.0 + jnp.exp(-x))


def _ada_kernel(c_ref, w_ref, b_ref, o_ref):
    c = c_ref[...]
    s = (c * _sigmoid(c)).astype(BF16)
    o_ref[...] = jnp.dot(s, w_ref[...].astype(BF16), preferred_element_type=F32) + b_ref[...]


def _ada(c8, w_ada, b_ada, tn=1024):
    rows, d = c8.shape
    n = w_ada.shape[1]
    return pl.pallas_call(
        _ada_kernel,
        grid=(n // tn,),
        in_specs=[pl.BlockSpec((rows, d), lambda j: (0, 0)),
                  pl.BlockSpec((d, tn), lambda j: (0, j)),
                  pl.BlockSpec((1, tn), lambda j: (0, j))],
        out_specs=pl.BlockSpec((rows, tn), lambda j: (0, j)),
        out_shape=jax.ShapeDtypeStruct((rows, n), F32),
        compiler_params=_params(("arbitrary",)),
        name="ada",
    )(c8, w_ada, b_ada)


def _inproj_kernel(x_ref, mod_ref, w_ref, bglu_ref, cos_ref, sina_ref, sinb_ref,
                   q_ref, k_ref, vt_ref, u_ref, *, attn_w, conv_w, chunk):
    mod = mod_ref[0]
    h = (_normalize(x_ref[...]) * (1.0 + mod[1:2]) + mod[0:1]).astype(BF16)
    cos, sina, sinb = cos_ref[...], sina_ref[...], sinb_ref[...]

    def proj(c0):
        return jnp.dot(h, w_ref[:, c0:c0 + chunk], preferred_element_type=F32)

    def rope(p):
        return p * cos + pltpu.roll(p, HEAD_W - 16, axis=1) * sina + pltpu.roll(p, 16, axis=1) * sinb

    for dst, base, scale in ((q_ref, 0, Q_SCALE), (k_ref, attn_w, None)):
        for j in range(attn_w // chunk):
            p = proj(base + j * chunk)
            for hh in range(chunk // HEAD_W):
                r = rope(p[:, hh * HEAD_W:(hh + 1) * HEAD_W])
                if scale is not None:
                    r = r * scale
                c0 = j * chunk + hh * HEAD_W
                dst[:, c0:c0 + HEAD_W] = r.astype(BF16)
    for j in range(attn_w // chunk):
        vt_ref[0, j * chunk:(j + 1) * chunk, :] = proj(2 * attn_w + j * chunk).T.astype(BF16)
    for j in range(conv_w // chunk):
        a = proj(3 * attn_w + j * chunk) + bglu_ref[:, j * chunk:(j + 1) * chunk]
        g = proj(3 * attn_w + conv_w + j * chunk) + bglu_ref[:, conv_w + j * chunk:conv_w + (j + 1) * chunk]
        u_ref[:, j * chunk:(j + 1) * chunk] = a * _sigmoid(g)


def _inproj(x2d, mod, w_in, b_glu, cos, sina, sinb, seq, attn_w, conv_w, tm=512, chunk=512):
    m, d = x2d.shape
    tiles_per_seq = seq // tm
    kern = functools.partial(_inproj_kernel, attn_w=attn_w, conv_w=conv_w, chunk=chunk)
    tab = pl.BlockSpec((tm, HEAD_W), lambda i: (i % tiles_per_seq, 0))
    row = lambda w: pl.BlockSpec((tm, w), lambda i: (i, 0))
    return pl.pallas_call(
        kern,
        grid=(m // tm,),
        in_specs=[row(d),
                  pl.BlockSpec((1,) + mod.shape[1:], lambda i: (i // tiles_per_seq, 0, 0)),
                  _resident(w_in.shape), _resident(b_glu.shape), tab, tab, tab],
        out_specs=[row(attn_w), row(attn_w),
                   pl.BlockSpec((1, attn_w, tm), lambda i: (i // tiles_per_seq, 0, i % tiles_per_seq)),
                   row(conv_w)],
        out_shape=[jax.ShapeDtypeStruct((m, attn_w), BF16)] * 2
                  + [jax.ShapeDtypeStruct((m // seq, attn_w, seq), BF16), jax.ShapeDtypeStruct((m, conv_w), F32)],
        compiler_params=_params(("arbitrary",)),
        name="inproj",
    )(x2d, mod, w_in, b_glu, cos, sina, sinb)


def _inproj_ctx_kernel(x_ref, mod_ref, wk_ref, wv_ref, k_ref, vt_ref, *, chunk):
    mod = mod_ref[0]
    h = (_normalize(x_ref[0]) * (1.0 + mod[1:2]) + mod[0:1]).astype(BF16)
    for j in range(wk_ref.shape[1] // chunk):
        sl = slice(j * chunk, (j + 1) * chunk)
        k_ref[0, :, sl] = jnp.dot(h, wk_ref[:, sl], preferred_element_type=F32).astype(BF16)
        vt_ref[0, sl, :] = jnp.dot(h, wv_ref[:, sl], preferred_element_type=F32).T.astype(BF16)


def _inproj_ctx(ctx, mod, w_in, ctx_row, attn_w, chunk=512):
    bsz, n_ctx, d = ctx.shape
    kern = functools.partial(_inproj_ctx_kernel, chunk=chunk)
    return pl.pallas_call(
        kern,
        grid=(bsz,),
        in_specs=[pl.BlockSpec((1, n_ctx, d), lambda i: (i, 0, 0)),
                  pl.BlockSpec((1,) + mod.shape[1:], lambda i: (ctx_row, 0, 0)),
                  pl.BlockSpec((d, attn_w), lambda i: (0, 1)),
                  pl.BlockSpec((d, attn_w), lambda i: (0, 2))],
        out_specs=[pl.BlockSpec((1, n_ctx, attn_w), lambda i: (i, 0, 0)),
                   pl.BlockSpec((1, attn_w, n_ctx), lambda i: (i, 0, 0))],
        out_shape=[jax.ShapeDtypeStruct((bsz, n_ctx, attn_w), BF16),
                   jax.ShapeDtypeStruct((bsz, attn_w, n_ctx), BF16)],
        compiler_params=_params(("arbitrary",)),
        name="inproj_ctx",
    )(ctx, mod, w_in, w_in)


def _attn_kernel(lamv_ref, g_ref, q_ref, k_ref, kc_ref, vt_ref, vct_ref, o_ref, s_ref, *, tq, tk):
    seq = k_ref.shape[1]
    n_ctx = kc_ref.shape[1]
    lv = lamv_ref[...]
    lam = (jnp.exp(jnp.sum(lv[0:1] * lv[1:2], axis=-1, keepdims=True))
           - jnp.exp(jnp.sum(lv[2:3] * lv[3:4], axis=-1, keepdims=True)) + LAM_INIT)
    lane = lax.broadcasted_iota(jnp.int32, (tq, HEAD_W), 1)
    chunks = ([(k_ref, vt_ref, o, o) for o in range(0, seq, tk)]
              + [(kc_ref, vct_ref, o, seq + o) for o in range(0, n_ctx, tk)])
    nt = (((1,), (1,)), ((), ()))

    def fold8(acc, x, op):
        for r in range(0, x.shape[0], 8):
            acc = op(acc, x[r:r + 8])
        return acc

    def q_half(tile_idx, c):
        q = q_ref[0, pl.ds(pl.multiple_of(tile_idx * tq, tq), tq), :]
        return jnp.where((lane >= c * HEAD_DIM) & (lane < (c + 1) * HEAD_DIM), q, jnp.zeros_like(q))

    def stage(c_out, m_out, c_in, q_in):
        m8 = jnp.full((8, tq), -jnp.inf, F32)
        l8 = jnp.zeros((8, tq), F32)
        acc = jnp.zeros((HEAD_W, tq), F32)
        for kr, vr, off, pos in chunks:
            st = lax.dot_general(kr[0, off:off + tk, :], q_in, nt, preferred_element_type=F32)
            s_ref[c_in, pos:pos + tk, :] = st
            m8 = fold8(m8, st, jnp.maximum)
            if c_out is not None:
                p = jnp.exp2(s_ref[c_out, pos:pos + tk, :] - m_out)
                l8 = fold8(l8, p, jnp.add)
                acc = acc + jnp.dot(vr[0, :, off:off + tk], p.astype(BF16), preferred_element_type=F32)
        out = None if c_out is None else acc / jnp.sum(l8, axis=0, keepdims=True)
        return out, jnp.max(m8, axis=0, keepdims=True)

    n_tiles = seq // tq

    def tile(i, m0):
        o0, m1 = stage(0, m0, 1, q_half(i, 1))
        o1, m0_next = stage(1, m1, 0, q_half(jnp.minimum(i + 1, n_tiles - 1), 0))
        ot = o0 - lam * o1
        ot = ot * lax.rsqrt(jnp.mean(ot * ot, axis=0, keepdims=True) + LN_EPS) * g_ref[...] * (1.0 - LAM_INIT)
        o_ref[0, pl.ds(pl.multiple_of(i * tq, tq), tq), :] = ot.T.astype(BF16)
        return m0_next

    _, m_first = stage(None, None, 0, q_half(0, 0))
    lax.fori_loop(0, n_tiles, tile, m_first)


def _attention(lamv, subln_g_col, q, k, kc, vt, vct, tq=256, tk=256):
    b, seq, attn_w = q.shape
    n_ctx = kc.shape[1]
    kern = functools.partial(_attn_kernel, tq=tq, tk=tk)
    head = lambda n: pl.BlockSpec((1, n, HEAD_W), lambda bi, hi: (bi, 0, hi))
    head_t = lambda n: pl.BlockSpec((1, HEAD_W, n), lambda bi, hi: (bi, hi, 0))
    return pl.pallas_call(
        kern,
        grid=(b, attn_w // HEAD_W),
        in_specs=[pl.BlockSpec(lamv.shape, lambda bi, hi: (0, 0)),
                  pl.BlockSpec(subln_g_col.shape, lambda bi, hi: (0, 0)),
                  head(seq), head(seq), head(n_ctx), head_t(seq), head_t(n_ctx)],
        out_specs=head(seq),
        out_shape=jax.ShapeDtypeStruct((b, seq, attn_w), BF16),
        scratch_shapes=[pltpu.VMEM((2, seq + n_ctx, tq), F32)],
        compiler_params=_params(("arbitrary", "arbitrary")),
        name="attn",
    )(lamv, subln_g_col, q, k, kc, vt, vct)


def _conv_kernel(u_ref, prev_ref, next_ref, w_ref, bdw_ref, g_ref, b_ref, o_ref, buf_ref, acc_ref,
                 *, tl, rows, lanes):
    i = pl.program_id(1)
    last = pl.num_programs(1) - 1
    width = u_ref.shape[2]
    buf_ref[HALO:HALO + tl, :] = u_ref[0]
    buf_ref[0:HALO, :] = jnp.where(i > 0, prev_ref[0], 0.0)
    buf_ref[HALO + tl:, :] = jnp.where(i < last, next_ref[0], 0.0)
    span = rows + 8 * ((CONV_KERNEL - 1) // 8)
    for r0 in range(0, tl, rows):
        for c0 in range(0, width, lanes):
            acc = jnp.zeros((rows, lanes), F32)
            for sub in range(8):
                start = r0 + HALO - CONV_PAD + sub
                xs = buf_ref[start:start + span, c0:c0 + lanes]
                for t in range(sub, CONV_KERNEL, 8):
                    acc = acc + w_ref[t:t + 1, c0:c0 + lanes] * xs[t - sub:t - sub + rows]
            acc_ref[r0:r0 + rows, c0:c0 + lanes] = acc + bdw_ref[:, c0:c0 + lanes]
    y = _normalize(acc_ref[...]) * g_ref[...] + b_ref[...]
    o_ref[0] = (y * _sigmoid(y)).astype(BF16)


def _conv(u, w_dw, b_dw, g, b, tl=512, rows=64, lanes=128):
    bsz, seq, width = u.shape
    halo_per_tile = tl // HALO
    n_halo = seq // HALO
    kern = functools.partial(_conv_kernel, tl=tl, rows=rows, lanes=lanes)
    vec = pl.BlockSpec((1, width), lambda bi, i: (0, 0))
    return pl.pallas_call(
        kern,
        grid=(bsz, seq // tl),
        in_specs=[pl.BlockSpec((1, tl, width), lambda bi, i: (bi, i, 0)),
                  pl.BlockSpec((1, HALO, width), lambda bi, i: (bi, jnp.maximum(i * halo_per_tile - 1, 0), 0)),
                  pl.BlockSpec((1, HALO, width),
                               lambda bi, i: (bi, jnp.minimum((i + 1) * halo_per_tile, n_halo - 1), 0)),
                  pl.BlockSpec(w_dw.shape, lambda bi, i: (0, 0)), vec, vec, vec],
        out_specs=pl.BlockSpec((1, tl, width), lambda bi, i: (bi, i, 0)),
        out_shape=jax.ShapeDtypeStruct((bsz, seq, width), BF16),
        scratch_shapes=[pltpu.VMEM((tl + 2 * HALO, width), F32), pltpu.VMEM((tl, width), F32)],
        compiler_params=_params(("arbitrary", "arbitrary")),
        name="conv",
    )(u, u, u, w_dw, b_dw, g, b)


def _outproj_kernel(a_ref, c_ref, x_ref, mod_ref, w_ref, bo_ref, g_ref, b_ref, o_ref, z_ref, *, chunk):
    attn_w = a_ref.shape[1]
    a = a_ref[...]
    cv = c_ref[...]
    gate = mod_ref[0, 2:3, :]
    for c0 in range(0, w_ref.shape[1], chunk):
        sl = slice(c0, c0 + chunk)
        y = (jnp.dot(a, w_ref[0:attn_w, sl], preferred_element_type=F32)
             + jnp.dot(cv, w_ref[attn_w:, sl], preferred_element_type=F32) + bo_ref[:, sl])
        z_ref[:, sl] = DEEPNORM_ALPHA * x_ref[:, sl] + gate[:, sl] * y
    o_ref[...] = _normalize(z_ref[...]) * g_ref[...] + b_ref[...]


def _outproj(attn2d, conv2d, x2d, mod, w_out, b_out, g, b, seq, tm=512, chunk=512):
    m, d = x2d.shape
    tiles_per_seq = seq // tm
    kern = functools.partial(_outproj_kernel, chunk=chunk)
    row = lambda w: pl.BlockSpec((tm, w), lambda i: (i, 0))
    vec = pl.BlockSpec((1, d), lambda i: (0, 0))
    return pl.pallas_call(
        kern,
        grid=(m // tm,),
        in_specs=[row(attn2d.shape[1]), row(conv2d.shape[1]), row(d),
                  pl.BlockSpec((1,) + mod.shape[1:], lambda i: (i // tiles_per_seq, 0, 0)),
                  _resident(w_out.shape), vec, vec, vec],
        out_specs=row(d),
        out_shape=jax.ShapeDtypeStruct((m, d), F32),
        scratch_shapes=[pltpu.VMEM((tm, d), F32)],
        compiler_params=_params(("arbitrary",)),
        name="outproj",
    )(attn2d, conv2d, x2d, mod, w_out, b_out, g, b)


def _ffn_kernel(x_ref, mod_ref, w1_ref, b1_ref, w2_ref, b2_ref, g_ref, b_ref, o_ref, h_ref):
    f = pl.program_id(1)
    mod = mod_ref[0]

    @pl.when(f == 0)
    def _():
        h_ref[...] = (_normalize(x_ref[...]) * (1.0 + mod[4:5]) + mod[3:4]).astype(BF16)
        o_ref[...] = jnp.zeros_like(o_ref)

    t = jnp.dot(h_ref[...], w1_ref[...], preferred_element_type=F32) + b1_ref[...]
    t = jnp.maximum(t, 0.0)
    o_ref[...] += jnp.dot((t * t).astype(BF16), w2_ref[...], preferred_element_type=F32)

    @pl.when(f == pl.num_programs(1) - 1)
    def _():
        z = DEEPNORM_ALPHA * x_ref[...] + mod[5:6] * (o_ref[...] + b2_ref[...])
        o_ref[...] = _normalize(z) * g_ref[...] + b_ref[...]


def _ffn(x2d, mod, w1, b1, w2, b2, g, b, seq, tm=1024, tf=512):
    m, d = x2d.shape
    dff = w1.shape[1]
    tiles_per_seq = seq // tm
    vec = pl.BlockSpec((1, d), lambda i, f: (0, 0))
    return pl.pallas_call(
        _ffn_kernel,
        grid=(m // tm, dff // tf),
        in_specs=[pl.BlockSpec((tm, d), lambda i, f: (i, 0)),
                  pl.BlockSpec((1,) + mod.shape[1:], lambda i, f: (i // tiles_per_seq, 0, 0)),
                  pl.BlockSpec((d, tf), lambda i, f: (0, f)),
                  pl.BlockSpec((1, tf), lambda i, f: (0, f)),
                  pl.BlockSpec((tf, d), lambda i, f: (f, 0)),
                  vec, vec, vec],
        out_specs=pl.BlockSpec((tm, d), lambda i, f: (i, 0)),
        out_shape=jax.ShapeDtypeStruct((m, d), F32),
        scratch_shapes=[pltpu.VMEM((tm, d), BF16)],
        compiler_params=_params(("arbitrary", "arbitrary")),
        name="ffn",
    )(x2d, mod, w1, b1, w2, b2, g, b)


def _rope_tables(seq):
    rows = seq // GRID_W
    row = jnp.repeat(jnp.arange(rows), GRID_W, total_repeat_length=seq).astype(F32)
    col = jnp.tile(jnp.arange(GRID_W), rows).astype(F32)
    axis_dim = HEAD_DIM // 2
    inv_freq = ROPE_BASE ** (-jnp.arange(0, axis_dim, 2, dtype=F32) / axis_dim)
    ang_r = row[:, None] * inv_freq
    ang_c = col[:, None] * inv_freq
    ang = jnp.concatenate([ang_r, ang_r, ang_c, ang_c] * 2, axis=-1)
    cos, sin = jnp.cos(ang), jnp.sin(ang)
    first = (jnp.arange(HEAD_W) % (2 * 16)) < 16
    return cos, jnp.where(first, -sin, 0.0), jnp.where(first, 0.0, sin)


def kernel(x, c, ctx, c_ctx, w_ada, b_ada, w_in, b_glu, lambda_q1, lambda_k1, lambda_q2, lambda_k2, subln_g, w_dw, b_dw, conv_ln_g, conv_ln_b, w_out, b_out, ln1_g, ln1_b, w_ff1, b_ff1, w_ff2, b_ff2, ln2_g, ln2_b):
    assert w_ada.shape[0] == DEPTH
    bsz, seq, d = x.shape
    n_ctx = ctx.shape[1]
    attn_w = N_HEADS * HEAD_W
    conv_w = d - attn_w
    assert w_in.shape[2] == 3 * attn_w + 2 * conv_w and seq % GRID_W == 0

    ctx_row = bsz
    c8 = jnp.concatenate([c, c_ctx[None, :], jnp.zeros((8 - bsz - 1, d), F32)], axis=0)
    mod = _ada(c8, w_ada[0], b_ada).reshape(8, 6, d)

    w_in_b = w_in[0].astype(BF16)
    cos, sina, sinb = _rope_tables(seq)
    x2d = x.reshape(bsz * seq, d)
    q, k, vt, u = _inproj(x2d, mod, w_in_b, b_glu, cos, sina, sinb, seq, attn_w, conv_w)
    kc, vct = _inproj_ctx(ctx, mod, w_in_b, ctx_row, attn_w)

    lamv = jnp.concatenate([lambda_q1, lambda_k1, lambda_q2, lambda_k2], axis=0)
    attn = _attention(lamv, subln_g.reshape(HEAD_W, 1), q.reshape(bsz, seq, attn_w),
                      k.reshape(bsz, seq, attn_w), kc, vt, vct)

    w_dw_p = jnp.concatenate([w_dw[0], jnp.zeros((1, conv_w), F32)], axis=0)
    conv = _conv(u.reshape(bsz, seq, conv_w), w_dw_p, b_dw, conv_ln_g, conv_ln_b)

    x1 = _outproj(attn.reshape(bsz * seq, attn_w), conv.reshape(bsz * seq, conv_w), x2d, mod,
                  w_out[0].astype(BF16), b_out, ln1_g, ln1_b, seq)
    out = _ffn(x1, mod, w_ff1[0].astype(BF16), b_ff1, w_ff2[0].astype(BF16), b_ff2, ln2_g, ln2_b, seq)
    return out.reshape(bsz, seq, d)
```

```python
import functools
import math

import jax
import jax.numpy as jnp
from jax import lax
from jax.experimental import pallas as pl
from jax.experimental.pallas import tpu as pltpu

F32 = jnp.float32
BF16 = jnp.bfloat16

GRID_W = 64
N_HEADS = 8
HEAD_DIM = 64
HEAD_W = 2 * HEAD_DIM
CONV_KERNEL = 31
CONV_PAD = CONV_KERNEL // 2
ROPE_BASE = 10000.0
LN_EPS = 1e-5
DEPTH = 1
DEEPNORM_ALPHA = (2.0 * DEPTH) ** 0.25
LAM_INIT = 0.8 - 0.6 * math.exp(-0.3 * 0)
Q_SCALE = HEAD_DIM ** -0.5 * math.log2(math.e)

V7X_VMEM_LIMIT = 60 * 1024 * 1024
HALO = 16


def _params(sem, vmem=V7X_VMEM_LIMIT):
    return pltpu.CompilerParams(dimension_semantics=sem, vmem_limit_bytes=vmem)


def _resident(shape):
    return pl.BlockSpec(shape, lambda *_: (0,) * len(shape), pipeline_mode=pl.Buffered(1))


def _normalize(x):
    mu = jnp.mean(x, axis=-1, keepdims=True)
    xc = x - mu
    var = jnp.mean(xc * xc, axis=-1, keepdims=True)
    return xc * lax.rsqrt(var + LN_EPS)


def _sigmoid(x):
    return 1.0 / (1.0 + jnp.exp(-x))


def _ada_kernel(c_ref, w_ref, b_ref, o_ref):
    c = c_ref[...]
    s = (c * _sigmoid(c)).astype(BF16)
    o_ref[...] = jnp.dot(s, w_ref[...].astype(BF16), preferred_element_type=F32) + b_ref[...]


def _ada(c8, w_ada, b_ada, tn=1024):
    rows, d = c8.shape
    n = w_ada.shape[1]
    return pl.pallas_call(
        _ada_kernel,
        grid=(n // tn,),
        in_specs=[pl.BlockSpec((rows, d), lambda j: (0, 0)),
                  pl.BlockSpec((d, tn), lambda j: (0, j)),
                  pl.BlockSpec((1, tn), lambda j: (0, j))],
        out_specs=pl.BlockSpec((rows, tn), lambda j: (0, j)),
        out_shape=jax.ShapeDtypeStruct((rows, n), F32),
        compiler_params=_params(("arbitrary",)),
        name="ada",
    )(c8, w_ada, b_ada)


def _inproj_kernel(x_ref, mod_ref, w_ref, bglu_ref, cos_ref, sina_ref, sinb_ref,
                   q_ref, k_ref, vt_ref, u_ref, *, attn_w, conv_w, chunk, rows):
    mod = mod_ref[0]
    for r0 in range(0, x_ref.shape[0], rows):
        rs = slice(r0, r0 + rows)
        h = (_normalize(x_ref[rs, :]) * (1.0 + mod[1:2]) + mod[0:1]).astype(BF16)
        cos, sina, sinb = cos_ref[rs, :], sina_ref[rs, :], sinb_ref[rs, :]

        def proj(c0):
            return jnp.dot(h, w_ref[:, c0:c0 + chunk], preferred_element_type=F32)

        def rope(p):
            return p * cos + pltpu.roll(p, HEAD_W - 16, axis=1) * sina + pltpu.roll(p, 16, axis=1) * sinb

        for dst, base, scale in ((q_ref, 0, Q_SCALE), (k_ref, attn_w, None)):
            for j in range(attn_w // chunk):
                p = proj(base + j * chunk)
                for hh in range(chunk // HEAD_W):
                    r = rope(p[:, hh * HEAD_W:(hh + 1) * HEAD_W])
                    if scale is not None:
                        r = r * scale
                    c0 = j * chunk + hh * HEAD_W
                    dst[rs, c0:c0 + HEAD_W] = r.astype(BF16)
        for j in range(attn_w // chunk):
            vt_ref[0, j * chunk:(j + 1) * chunk, rs] = proj(2 * attn_w + j * chunk).T.astype(BF16)
        for j in range(conv_w // chunk):
            a = proj(3 * attn_w + j * chunk) + bglu_ref[:, j * chunk:(j + 1) * chunk]
            g = proj(3 * attn_w + conv_w + j * chunk) + bglu_ref[:, conv_w + j * chunk:conv_w + (j + 1) * chunk]
            u_ref[rs, j * chunk:(j + 1) * chunk] = a * _sigmoid(g)


def _inproj(x2d, mod, w_in, b_glu, cos, sina, sinb, seq, attn_w, conv_w, tm=512, chunk=512, rows=256):
    m, d = x2d.shape
    tiles_per_seq = seq // tm
    kern = functools.partial(_inproj_kernel, attn_w=attn_w, conv_w=conv_w, chunk=chunk, rows=rows)
    tab = pl.BlockSpec((tm, HEAD_W), lambda i: (i % tiles_per_seq, 0))
    row = lambda w: pl.BlockSpec((tm, w), lambda i: (i, 0))
    return pl.pallas_call(
        kern,
        grid=(m // tm,),
        in_specs=[row(d),
                  pl.BlockSpec((1,) + mod.shape[1:], lambda i: (i // tiles_per_seq, 0, 0)),
                  _resident(w_in.shape), _resident(b_glu.shape), tab, tab, tab],
        out_specs=[row(attn_w), row(attn_w),
                   pl.BlockSpec((1, attn_w, tm), lambda i: (i // tiles_per_seq, 0, i % tiles_per_seq)),
                   row(conv_w)],
        out_shape=[jax.ShapeDtypeStruct((m, attn_w), BF16)] * 2
                  + [jax.ShapeDtypeStruct((m // seq, attn_w, seq), BF16), jax.ShapeDtypeStruct((m, conv_w), F32)],
        compiler_params=_params(("arbitrary",)),
        name="inproj",
    )(x2d, mod, w_in, b_glu, cos, sina, sinb)


def _inproj_ctx_kernel(x_ref, mod_ref, wk_ref, wv_ref, k_ref, vt_ref, *, chunk):
    mod = mod_ref[0]
    h = (_normalize(x_ref[0]) * (1.0 + mod[1:2]) + mod[0:1]).astype(BF16)
    for j in range(wk_ref.shape[1] // chunk):
        sl = slice(j * chunk, (j + 1) * chunk)
        k_ref[0, :, sl] = jnp.dot(h, wk_ref[:, sl], preferred_element_type=F32).astype(BF16)
        vt_ref[0, sl, :] = jnp.dot(h, wv_ref[:, sl], preferred_element_type=F32).T.astype(BF16)


def _inproj_ctx(ctx, mod, w_in, ctx_row, attn_w, chunk=512):
    bsz, n_ctx, d = ctx.shape
    kern = functools.partial(_inproj_ctx_kernel, chunk=chunk)
    return pl.pallas_call(
        kern,
        grid=(bsz,),
        in_specs=[pl.BlockSpec((1, n_ctx, d), lambda i: (i, 0, 0)),
                  pl.BlockSpec((1,) + mod.shape[1:], lambda i: (ctx_row, 0, 0)),
                  pl.BlockSpec((d, attn_w), lambda i: (0, 1)),
                  pl.BlockSpec((d, attn_w), lambda i: (0, 2))],
        out_specs=[pl.BlockSpec((1, n_ctx, attn_w), lambda i: (i, 0, 0)),
                   pl.BlockSpec((1, attn_w, n_ctx), lambda i: (i, 0, 0))],
        out_shape=[jax.ShapeDtypeStruct((bsz, n_ctx, attn_w), BF16),
                   jax.ShapeDtypeStruct((bsz, attn_w, n_ctx), BF16)],
        compiler_params=_params(("arbitrary",)),
        name="inproj_ctx",
    )(ctx, mod, w_in, w_in)


def _attn_kernel(lamv_ref, g_ref, q_ref, k_ref, kc_ref, vt_ref, vct_ref, o_ref, s_ref, *, tq, tk, unroll):
    seq = k_ref.shape[1]
    n_ctx = kc_ref.shape[1]
    lv = lamv_ref[...]
    lam = (jnp.exp(jnp.sum(lv[0:1] * lv[1:2], axis=-1, keepdims=True))
           - jnp.exp(jnp.sum(lv[2:3] * lv[3:4], axis=-1, keepdims=True)) + LAM_INIT)
    lane = lax.broadcasted_iota(jnp.int32, (tq, HEAD_W), 1)
    tkc = min(tk, n_ctx)
    chunks = ([(k_ref, vt_ref, o, tk, o) for o in range(0, seq, tk)]
              + [(kc_ref, vct_ref, o, tkc, seq + o) for o in range(0, n_ctx, tkc)])
    nt = (((1,), (1,)), ((), ()))

    def fold8(acc, x, op):
        for r in range(0, x.shape[0], 8):
            acc = op(acc, x[r:r + 8])
        return acc

    def q_half(tile_idx, c):
        q = q_ref[0, pl.ds(pl.multiple_of(tile_idx * tq, tq), tq), :]
        return jnp.where((lane >= c * HEAD_DIM) & (lane < (c + 1) * HEAD_DIM), q, jnp.zeros_like(q))

    def stage(c_out, m_out, c_in, q_in):
        m8 = jnp.full((8, tq), -jnp.inf, F32)
        l8 = jnp.zeros((8, tq), F32)
        acc = jnp.zeros((HEAD_W, tq), F32)
        for kr, vr, off, n, pos in chunks:
            st = lax.dot_general(kr[0, off:off + n, :], q_in, nt, preferred_element_type=F32)
            s_ref[c_in, pos:pos + n, :] = st
            m8 = fold8(m8, st, jnp.maximum)
            if c_out is not None:
                p = jnp.exp2(s_ref[c_out, pos:pos + n, :] - m_out)
                l8 = fold8(l8, p, jnp.add)
                acc = acc + jnp.dot(vr[0, :, off:off + n], p.astype(BF16), preferred_element_type=F32)
        out = None if c_out is None else acc / jnp.sum(l8, axis=0, keepdims=True)
        return out, jnp.max(m8, axis=0, keepdims=True)

    n_tiles = seq // tq

    def tile(i, m0):
        o0, m1 = stage(0, m0, 1, q_half(i, 1))
        o1, m0_next = stage(1, m1, 0, q_half(jnp.minimum(i + 1, n_tiles - 1), 0))
        ot = o0 - lam * o1
        ot = ot * lax.rsqrt(jnp.mean(ot * ot, axis=0, keepdims=True) + LN_EPS) * g_ref[...] * (1.0 - LAM_INIT)
        o_ref[0, pl.ds(pl.multiple_of(i * tq, tq), tq), :] = ot.T.astype(BF16)
        return m0_next

    _, m_first = stage(None, None, 0, q_half(0, 0))
    lax.fori_loop(0, n_tiles, tile, m_first, unroll=unroll)


def _attention(lamv, subln_g_col, q, k, kc, vt, vct, tq=256, tk=256, unroll=1):
    b, seq, attn_w = q.shape
    n_ctx = kc.shape[1]
    kern = functools.partial(_attn_kernel, tq=tq, tk=tk, unroll=unroll)
    head = lambda n: pl.BlockSpec((1, n, HEAD_W), lambda bi, hi: (bi, 0, hi))
    head_t = lambda n: pl.BlockSpec((1, HEAD_W, n), lambda bi, hi: (bi, hi, 0))
    return pl.pallas_call(
        kern,
        grid=(b, attn_w // HEAD_W),
        in_specs=[pl.BlockSpec(lamv.shape, lambda bi, hi: (0, 0)),
                  pl.BlockSpec(subln_g_col.shape, lambda bi, hi: (0, 0)),
                  head(seq), head(seq), head(n_ctx), head_t(seq), head_t(n_ctx)],
        out_specs=head(seq),
        out_shape=jax.ShapeDtypeStruct((b, seq, attn_w), BF16),
        scratch_shapes=[pltpu.VMEM((2, seq + n_ctx, tq), F32)],
        compiler_params=_params(("arbitrary", "arbitrary")),
        name="attn",
    )(lamv, subln_g_col, q, k, kc, vt, vct)


def _conv_kernel(u_ref, prev_ref, next_ref, w_ref, bdw_ref, g_ref, b_ref, o_ref, buf_ref, acc_ref,
                 *, tl, rows, lanes):
    i = pl.program_id(1)
    last = pl.num_programs(1) - 1
    width = u_ref.shape[2]
    buf_ref[HALO:HALO + tl, :] = u_ref[0]
    buf_ref[0:HALO, :] = jnp.where(i > 0, prev_ref[0], 0.0)
    buf_ref[HALO + tl:, :] = jnp.where(i < last, next_ref[0], 0.0)
    span = rows + 8 * ((CONV_KERNEL - 1) // 8)
    win = span + 8
    def row_block(rb, carry):
        r0 = pl.multiple_of(rb * rows, rows)
        for c0 in range(0, width, lanes):
            acc = jnp.zeros((rows, lanes), F32)
            xa = buf_ref[pl.ds(r0, win), c0:c0 + lanes]
            for sub in range(8):
                shift = HALO - CONV_PAD + sub
                xs = pltpu.roll(xa, win - shift, axis=0) if shift % 8 else xa[shift:]
                for t in range(sub, CONV_KERNEL, 8):
                    acc = acc + w_ref[t:t + 1, c0:c0 + lanes] * xs[t - sub:t - sub + rows]
            acc_ref[pl.ds(r0, rows), c0:c0 + lanes] = acc + bdw_ref[:, c0:c0 + lanes]
        return carry

    lax.fori_loop(0, tl // rows, row_block, 0)
    y = _normalize(acc_ref[...]) * g_ref[...] + b_ref[...]
    o_ref[0] = (y * _sigmoid(y)).astype(BF16)


def _conv(u, w_dw, b_dw, g, b, tl=512, rows=64, lanes=128):
    bsz, seq, width = u.shape
    halo_per_tile = tl // HALO
    n_halo = seq // HALO
    kern = functools.partial(_conv_kernel, tl=tl, rows=rows, lanes=lanes)
    vec = pl.BlockSpec((1, width), lambda bi, i: (0, 0))
    return pl.pallas_call(
        kern,
        grid=(bsz, seq // tl),
        in_specs=[pl.BlockSpec((1, tl, width), lambda bi, i: (bi, i, 0)),
                  pl.BlockSpec((1, HALO, width), lambda bi, i: (bi, jnp.maximum(i * halo_per_tile - 1, 0), 0)),
                  pl.BlockSpec((1, HALO, width),
                               lambda bi, i: (bi, jnp.minimum((i + 1) * halo_per_tile, n_halo - 1), 0)),
                  pl.BlockSpec(w_dw.shape, lambda bi, i: (0, 0)), vec, vec, vec],
        out_specs=pl.BlockSpec((1, tl, width), lambda bi, i: (bi, i, 0)),
        out_shape=jax.ShapeDtypeStruct((bsz, seq, width), BF16),
        scratch_shapes=[pltpu.VMEM((tl + 2 * HALO, width), F32), pltpu.VMEM((tl, width), F32)],
        compiler_params=_params(("arbitrary", "arbitrary")),
        name="conv",
    )(u, u, u, w_dw, b_dw, g, b)


def _outproj_kernel(a_ref, c_ref, x_ref, mod_ref, w_ref, bo_ref, g_ref, b_ref, o_ref, z_ref, *, chunk, rows):
    attn_w = a_ref.shape[1]
    gate = mod_ref[0, 2:3, :]
    for r0 in range(0, x_ref.shape[0], rows):
        rs = slice(r0, r0 + rows)
        a = a_ref[rs, :]
        cv = c_ref[rs, :]
        for c0 in range(0, w_ref.shape[1], chunk):
            sl = slice(c0, c0 + chunk)
            y = (jnp.dot(a, w_ref[0:attn_w, sl], preferred_element_type=F32)
                 + jnp.dot(cv, w_ref[attn_w:, sl], preferred_element_type=F32) + bo_ref[:, sl])
            z_ref[rs, sl] = DEEPNORM_ALPHA * x_ref[rs, sl] + gate[:, sl] * y
        o_ref[rs, :] = _normalize(z_ref[rs, :]) * g_ref[...] + b_ref[...]


def _outproj(attn2d, conv2d, x2d, mod, w_out, b_out, g, b, seq, tm=512, chunk=512, rows=256):
    m, d = x2d.shape
    tiles_per_seq = seq // tm
    kern = functools.partial(_outproj_kernel, chunk=chunk, rows=rows)
    row = lambda w: pl.BlockSpec((tm, w), lambda i: (i, 0))
    vec = pl.BlockSpec((1, d), lambda i: (0, 0))
    return pl.pallas_call(
        kern,
        grid=(m // tm,),
        in_specs=[row(attn2d.shape[1]), row(conv2d.shape[1]), row(d),
                  pl.BlockSpec((1,) + mod.shape[1:], lambda i: (i // tiles_per_seq, 0, 0)),
                  _resident(w_out.shape), vec, vec, vec],
        out_specs=row(d),
        out_shape=jax.ShapeDtypeStruct((m, d), F32),
        scratch_shapes=[pltpu.VMEM((tm, d), F32)],
        compiler_params=_params(("arbitrary",)),
        name="outproj",
    )(attn2d, conv2d, x2d, mod, w_out, b_out, g, b)


def _ffn_kernel(x_ref, mod_ref, w1_ref, b1_ref, w2_ref, b2_ref, g_ref, b_ref, o_ref, h_ref, *, rows):
    f = pl.program_id(1)
    last = pl.num_programs(1) - 1
    mod = mod_ref[0]

    def step(first, final):
        for r0 in range(0, x_ref.shape[0], rows):
            rs = slice(r0, r0 + rows)
            if first:
                h = (_normalize(x_ref[rs, :]) * (1.0 + mod[4:5]) + mod[3:4]).astype(BF16)
                h_ref[rs, :] = h
            else:
                h = h_ref[rs, :]
            t = jnp.maximum(jnp.dot(h, w1_ref[...], preferred_element_type=F32) + b1_ref[...], 0.0)
            y = jnp.dot((t * t).astype(BF16), w2_ref[...], preferred_element_type=F32)
            if not first:
                y = o_ref[rs, :] + y
            if final:
                z = DEEPNORM_ALPHA * x_ref[rs, :] + mod[5:6] * (y + b2_ref[...])
                y = _normalize(z) * g_ref[...] + b_ref[...]
            o_ref[rs, :] = y

    pl.when(f == 0)(functools.partial(step, True, False))
    pl.when((f > 0) & (f < last))(functools.partial(step, False, False))
    pl.when(f == last)(functools.partial(step, False, True))


def _ffn(x2d, mod, w1, b1, w2, b2, g, b, seq, tm=1024, tf=1024, rows=512):
    m, d = x2d.shape
    dff = w1.shape[1]
    tiles_per_seq = seq // tm
    assert dff // tf >= 2
    vec = pl.BlockSpec((1, d), lambda i, f: (0, 0))
    return pl.pallas_call(
        functools.partial(_ffn_kernel, rows=rows),
        grid=(m // tm, dff // tf),
        in_specs=[pl.BlockSpec((tm, d), lambda i, f: (i, 0)),
                  pl.BlockSpec((1,) + mod.shape[1:], lambda i, f: (i // tiles_per_seq, 0, 0)),
                  pl.BlockSpec((d, tf), lambda i, f: (0, f)),
                  pl.BlockSpec((1, tf), lambda i, f: (0, f)),
                  pl.BlockSpec((tf, d), lambda i, f: (f, 0)),
                  vec, vec, vec],
        out_specs=pl.BlockSpec((tm, d), lambda i, f: (i, 0)),
        out_shape=jax.ShapeDtypeStruct((m, d), F32),
        scratch_shapes=[pltpu.VMEM((tm, d), BF16)],
        compiler_params=_params(("arbitrary", "arbitrary")),
        name="ffn",
    )(x2d, mod, w1, b1, w2, b2, g, b)


def _rope_tables(seq):
    rows = seq // GRID_W
    axis_dim = HEAD_DIM // 2
    nf = axis_dim // 2
    inv_freq = ROPE_BASE ** (-jnp.arange(0, axis_dim, 2, dtype=F32) / axis_dim)
    ang_r = jnp.arange(rows, dtype=F32)[:, None] * inv_freq
    ang_c = jnp.arange(GRID_W, dtype=F32)[:, None] * inv_freq

    def expand(fn):
        r = jnp.broadcast_to(fn(ang_r)[:, None, :], (rows, GRID_W, nf)).reshape(seq, nf)
        c = jnp.broadcast_to(fn(ang_c)[None, :, :], (rows, GRID_W, nf)).reshape(seq, nf)
        return jnp.concatenate([r, r, c, c] * 2, axis=-1)

    cos, sin = expand(jnp.cos), expand(jnp.sin)
    first = (jnp.arange(HEAD_W) % (2 * nf)) < nf
    return cos, jnp.where(first, -sin, 0.0), jnp.where(first, 0.0, sin)


def kernel(x, c, ctx, c_ctx, w_ada, b_ada, w_in, b_glu, lambda_q1, lambda_k1, lambda_q2, lambda_k2, subln_g, w_dw, b_dw, conv_ln_g, conv_ln_b, w_out, b_out, ln1_g, ln1_b, w_ff1, b_ff1, w_ff2, b_ff2, ln2_g, ln2_b):
    assert w_ada.shape[0] == DEPTH
    bsz, seq, d = x.shape
    n_ctx = ctx.shape[1]
    attn_w = N_HEADS * HEAD_W
    conv_w = d - attn_w
    assert w_in.shape[2] == 3 * attn_w + 2 * conv_w and seq % GRID_W == 0

    ctx_row = bsz
    c8 = jnp.concatenate([c, c_ctx[None, :], jnp.zeros((8 - bsz - 1, d), F32)], axis=0)
    mod = _ada(c8, w_ada[0], b_ada).reshape(8, 6, d)

    w_in_b = w_in[0].astype(BF16)
    cos, sina, sinb = _rope_tables(seq)
    x2d = x.reshape(bsz * seq, d)
    q, k, vt, u = _inproj(x2d, mod, w_in_b, b_glu, cos, sina, sinb, seq, attn_w, conv_w)
    kc, vct = _inproj_ctx(ctx, mod, w_in_b, ctx_row, attn_w)

    lamv = jnp.concatenate([lambda_q1, lambda_k1, lambda_q2, lambda_k2], axis=0)
    attn = _attention(lamv, subln_g.reshape(HEAD_W, 1), q.reshape(bsz, seq, attn_w),
                      k.reshape(bsz, seq, attn_w), kc, vt, vct)

    w_dw_p = jnp.concatenate([w_dw[0], jnp.zeros((1, conv_w), F32)], axis=0)
    conv = _conv(u.reshape(bsz, seq, conv_w), w_dw_p, b_dw, conv_ln_g, conv_ln_b)

    x1 = _outproj(attn.reshape(bsz * seq, attn_w), conv.reshape(bsz * seq, conv_w), x2d, mod,
                  w_out[0].astype(BF16), b_out, ln1_g, ln1_b, seq)
    out = _ffn(x1, mod, w_ff1[0].astype(BF16), b_ff1, w_ff2[0].astype(BF16), b_ff2, ln2_g, ln2_b, seq)
    return out.reshape(bsz, seq, d)
```

```python
import functools
import math

import jax
import jax.numpy as jnp
from jax import lax
from jax.experimental import pallas as pl
from jax.experimental.pallas import tpu as pltpu

F32 = jnp.float32
BF16 = jnp.bfloat16

GRID_W = 64
N_HEADS = 8
HEAD_DIM = 64
HEAD_W = 2 * HEAD_DIM
V_PAD_ROWS = 16
V_ROWS = HEAD_W + V_PAD_ROWS
CONV_KERNEL = 31
CONV_PAD = CONV_KERNEL // 2
ROPE_BASE = 10000.0
LN_EPS = 1e-5
DEPTH = 1
DEEPNORM_ALPHA = (2.0 * DEPTH) ** 0.25
LAM_INIT = 0.8 - 0.6 * math.exp(-0.3 * 0)
Q_SCALE = HEAD_DIM ** -0.5 * math.log2(math.e)

V7X_VMEM_LIMIT = 60 * 1024 * 1024
HALO = 16
CONV_IN_STAGE = 4


def _params(sem, vmem=V7X_VMEM_LIMIT):
    return pltpu.CompilerParams(dimension_semantics=sem, vmem_limit_bytes=vmem)


def _resident(shape):
    return pl.BlockSpec(shape, lambda *_: (0,) * len(shape), pipeline_mode=pl.Buffered(1))


def _normalize(x):
    mu = jnp.mean(x, axis=-1, keepdims=True)
    xc = x - mu
    var = jnp.mean(xc * xc, axis=-1, keepdims=True)
    return xc * lax.rsqrt(var + LN_EPS)


def _sigmoid(x):
    return 1.0 / (1.0 + jnp.exp(-x))


def _ones_row_block(n):
    row = lax.broadcasted_iota(jnp.int32, (V_PAD_ROWS, n), 0)
    return jnp.where(row == 0, 1.0, 0.0).astype(BF16)


def _ada_kernel(c_ref, w_ref, b_ref, o_ref):
    c = c_ref[...]
    s = (c * _sigmoid(c)).astype(BF16)
    o_ref[...] = jnp.dot(s, w_ref[...].astype(BF16), preferred_element_type=F32) + b_ref[...]


def _ada(c8, w_ada, b_ada, tn=1024):
    rows, d = c8.shape
    n = w_ada.shape[1]
    return pl.pallas_call(
        _ada_kernel,
        grid=(n // tn,),
        in_specs=[pl.BlockSpec((rows, d), lambda j: (0, 0)),
                  pl.BlockSpec((d, tn), lambda j: (0, j)),
                  pl.BlockSpec((1, tn), lambda j: (0, j))],
        out_specs=pl.BlockSpec((rows, tn), lambda j: (0, j)),
        out_shape=jax.ShapeDtypeStruct((rows, n), F32),
        compiler_params=_params(("arbitrary",)),
        name="ada",
    )(c8, w_ada, b_ada)


def _inproj_kernel(x_ref, mod_ref, w_ref, bglu_ref, cos_ref, sina_ref, sinb_ref,
                   q_ref, k_ref, vt_ref, u_ref, *, attn_w, conv_w, chunk, rows):
    mod = mod_ref[0]
    for r0 in range(0, x_ref.shape[0], rows):
        rs = slice(r0, r0 + rows)
        h = (_normalize(x_ref[rs, :]) * (1.0 + mod[1:2]) + mod[0:1]).astype(BF16)
        cos, sina, sinb = cos_ref[rs, :], sina_ref[rs, :], sinb_ref[rs, :]

        def proj(c0):
            return jnp.dot(h, w_ref[:, c0:c0 + chunk], preferred_element_type=F32)

        def rope(p):
            return p * cos + pltpu.roll(p, HEAD_W - 16, axis=1) * sina + pltpu.roll(p, 16, axis=1) * sinb

        for dst, base, scale in ((q_ref, 0, Q_SCALE), (k_ref, attn_w, None)):
            for j in range(attn_w // chunk):
                p = proj(base + j * chunk)
                for hh in range(chunk // HEAD_W):
                    r = rope(p[:, hh * HEAD_W:(hh + 1) * HEAD_W])
                    if scale is not None:
                        r = r * scale
                    c0 = j * chunk + hh * HEAD_W
                    dst[rs, c0:c0 + HEAD_W] = r.astype(BF16)
        for j in range(attn_w // chunk):
            vt = proj(2 * attn_w + j * chunk).T.astype(BF16)
            for hh in range(chunk // HEAD_W):
                head = j * (chunk // HEAD_W) + hh
                vt_ref[0, head, 0:HEAD_W, rs] = vt[hh * HEAD_W:(hh + 1) * HEAD_W]
                vt_ref[0, head, HEAD_W:, rs] = _ones_row_block(rows)
        for j in range(conv_w // chunk):
            a = proj(3 * attn_w + j * chunk) + bglu_ref[:, j * chunk:(j + 1) * chunk]
            g = proj(3 * attn_w + conv_w + j * chunk) + bglu_ref[:, conv_w + j * chunk:conv_w + (j + 1) * chunk]
            u_ref[rs, j * chunk:(j + 1) * chunk] = a * _sigmoid(g)


def _inproj(x2d, mod, w_in, b_glu, cos, sina, sinb, seq, attn_w, conv_w, tm=512, chunk=512, rows=256):
    m, d = x2d.shape
    tiles_per_seq = seq // tm
    kern = functools.partial(_inproj_kernel, attn_w=attn_w, conv_w=conv_w, chunk=chunk, rows=rows)
    tab = pl.BlockSpec((tm, HEAD_W), lambda i: (i % tiles_per_seq, 0))
    row = lambda w: pl.BlockSpec((tm, w), lambda i: (i, 0))
    return pl.pallas_call(
        kern,
        grid=(m // tm,),
        in_specs=[row(d),
                  pl.BlockSpec((1,) + mod.shape[1:], lambda i: (i // tiles_per_seq, 0, 0)),
                  _resident(w_in.shape), _resident(b_glu.shape), tab, tab, tab],
        out_specs=[row(attn_w), row(attn_w),
                   pl.BlockSpec((1, N_HEADS, V_ROWS, tm), lambda i: (i // tiles_per_seq, 0, 0, i % tiles_per_seq)),
                   row(conv_w)],
        out_shape=[jax.ShapeDtypeStruct((m, attn_w), BF16)] * 2
                  + [jax.ShapeDtypeStruct((m // seq, N_HEADS, V_ROWS, seq), BF16),
                     jax.ShapeDtypeStruct((m, conv_w), F32)],
        compiler_params=_params(("arbitrary",)),
        name="inproj",
    )(x2d, mod, w_in, b_glu, cos, sina, sinb)


def _inproj_ctx_kernel(x_ref, mod_ref, wk_ref, wv_ref, k_ref, vt_ref, *, chunk):
    mod = mod_ref[0]
    h = (_normalize(x_ref[0]) * (1.0 + mod[1:2]) + mod[0:1]).astype(BF16)
    for j in range(wk_ref.shape[1] // chunk):
        sl = slice(j * chunk, (j + 1) * chunk)
        k_ref[0, :, sl] = jnp.dot(h, wk_ref[:, sl], preferred_element_type=F32).astype(BF16)
        vt = jnp.dot(h, wv_ref[:, sl], preferred_element_type=F32).T.astype(BF16)
        for hh in range(chunk // HEAD_W):
            head = j * (chunk // HEAD_W) + hh
            vt_ref[0, head, 0:HEAD_W, :] = vt[hh * HEAD_W:(hh + 1) * HEAD_W]
            vt_ref[0, head, HEAD_W:, :] = _ones_row_block(h.shape[0])


def _inproj_ctx(ctx, mod, w_in, ctx_row, attn_w, chunk=512):
    bsz, n_ctx, d = ctx.shape
    kern = functools.partial(_inproj_ctx_kernel, chunk=chunk)
    return pl.pallas_call(
        kern,
        grid=(bsz,),
        in_specs=[pl.BlockSpec((1, n_ctx, d), lambda i: (i, 0, 0)),
                  pl.BlockSpec((1,) + mod.shape[1:], lambda i: (ctx_row, 0, 0)),
                  pl.BlockSpec((d, attn_w), lambda i: (0, 1)),
                  pl.BlockSpec((d, attn_w), lambda i: (0, 2))],
        out_specs=[pl.BlockSpec((1, n_ctx, attn_w), lambda i: (i, 0, 0)),
                   pl.BlockSpec((1, N_HEADS, V_ROWS, n_ctx), lambda i: (i, 0, 0, 0))],
        out_shape=[jax.ShapeDtypeStruct((bsz, n_ctx, attn_w), BF16),
                   jax.ShapeDtypeStruct((bsz, N_HEADS, V_ROWS, n_ctx), BF16)],
        compiler_params=_params(("arbitrary",)),
        name="inproj_ctx",
    )(ctx, mod, w_in, w_in)


def _attn_kernel(lamv_ref, g_ref, q_ref, k_ref, kc_ref, vt_ref, vct_ref, u_ref, wdw_ref, bdw_ref,
                 wo_ref, w1_ref, w2_ref, o_ref, cv_ref, wo16_ref, w116_ref, w216_ref, s_ref, ubuf_ref,
                 *, tq, tk, unroll, conv_rows):
    seq = k_ref.shape[1]
    n_ctx = kc_ref.shape[1]
    for src, dst in ((wo_ref, wo16_ref), (w1_ref, w116_ref), (w2_ref, w216_ref)):
        dst[...] = src[...].astype(BF16)

    ubuf_ref[0:HALO, :] = jnp.zeros((HALO, HEAD_W), F32)
    ubuf_ref[HALO + seq:, :] = jnp.zeros((HALO, HEAD_W), F32)
    ubuf_ref[HALO:HALO + seq, :] = u_ref[0]
    span = conv_rows + 8 * ((CONV_KERNEL - 1) // 8)
    win = span + 8

    def conv_block(r0, after):
        zero = lax.shift_right_logical(lax.shift_right_logical(after.astype(jnp.int32), 16), 16).astype(F32)
        xa = ubuf_ref[pl.ds(r0, win), :] + jnp.concatenate([zero] * (win // 8), axis=0)
        acc = jnp.zeros((conv_rows, HEAD_W), F32)
        for sub in range(8):
            shift = HALO - CONV_PAD + sub
            xs = pltpu.roll(xa, win - shift, axis=0) if shift % 8 else xa[shift:]
            for t in range(sub, CONV_KERNEL, 8):
                acc = acc + wdw_ref[t:t + 1, :] * xs[t - sub:t - sub + conv_rows]
        cv_ref[0, pl.ds(r0, conv_rows), :] = acc + bdw_ref[...]
    lv = lamv_ref[...]
    lam = (jnp.exp(jnp.sum(lv[0:1] * lv[1:2], axis=-1, keepdims=True))
           - jnp.exp(jnp.sum(lv[2:3] * lv[3:4], axis=-1, keepdims=True)) + LAM_INIT)
    lane = lax.broadcasted_iota(jnp.int32, (tq, HEAD_W), 1)
    tkc = min(tk, n_ctx)
    chunks = ([(k_ref, vt_ref, o, tk, o) for o in range(0, seq, tk)]
              + [(kc_ref, vct_ref, o, tkc, seq + o) for o in range(0, n_ctx, tkc)])
    nt = (((1,), (1,)), ((), ()))

    def fold8(acc, x, op):
        parts = [x[r:r + 8] for r in range(0, x.shape[0], 8)]
        while len(parts) > 1:
            parts = [op(parts[a], parts[a + 1]) for a in range(0, len(parts), 2)]
        return op(acc, parts[0])

    def q_half(tile_idx, c):
        q = q_ref[0, pl.ds(pl.multiple_of(tile_idx * tq, tq), tq), :]
        return jnp.where((lane >= c * HEAD_DIM) & (lane < (c + 1) * HEAD_DIM), q, jnp.zeros_like(q))

    def stage(c_out, m_out, c_in, q_in, conv_sites=()):
        m8 = jnp.full((8, tq), -jnp.inf, F32)
        acc = jnp.zeros((V_ROWS, tq), F32)
        for j, (kr, vr, off, n, pos) in enumerate(chunks):
            if c_out is not None:
                p = jnp.exp2(s_ref[c_out, pos:pos + n, :] - m_out).astype(BF16)
                acc = acc + jnp.dot(vr[0, 0, :, off:off + n], p, preferred_element_type=F32)
            st = lax.dot_general(kr[0, off:off + n, :], q_in, nt, preferred_element_type=F32)
            s_ref[c_in, pos:pos + n, :] = st
            m8 = fold8(m8, st, jnp.maximum)
            for site, r0 in conv_sites:
                if site == j:
                    conv_block(r0, st[0:8, 0:HEAD_W])
        out = None if c_out is None else acc[0:HEAD_W] / acc[HEAD_W:HEAD_W + 1]
        return out, jnp.max(m8, axis=0, keepdims=True)

    n_tiles = seq // tq
    n_blocks = tq // conv_rows
    last = len(chunks) - 1
    every = len(chunks) // CONV_IN_STAGE
    sites_a = [every * b + every // 2 for b in range(CONV_IN_STAGE)]
    sites_b = sites_a + [last] * (n_blocks - 2 * CONV_IN_STAGE)

    def tile(i, m0):
        starts = [pl.multiple_of(i * tq + b * conv_rows, conv_rows) for b in range(n_blocks)]
        o0, m1 = stage(0, m0, 1, q_half(i, 1), list(zip(sites_a, starts)))
        o1, m0_next = stage(1, m1, 0, q_half(jnp.minimum(i + 1, n_tiles - 1), 0),
                            list(zip(sites_b, starts[CONV_IN_STAGE:])))
        ot = o0 - lam * o1
        ot = ot * lax.rsqrt(jnp.mean(ot * ot, axis=0, keepdims=True) + LN_EPS) * g_ref[...] * (1.0 - LAM_INIT)
        o_ref[0, pl.ds(pl.multiple_of(i * tq, tq), tq), :] = ot.T.astype(BF16)
        return m0_next

    _, m_first = stage(None, None, 0, q_half(0, 0))
    lax.fori_loop(0, n_tiles, tile, m_first, unroll=unroll)


def _attention(lamv, subln_g_col, q, k, kc, vt, vct, u, w_dw, b_dw, cast_weights,
               tq=256, tk=256, unroll=1, conv_rows=32):
    assert len(cast_weights) == 3
    b, seq, attn_w = q.shape
    n_ctx = kc.shape[1]
    assert u.shape[2] == attn_w and w_dw.shape[0] % 8 == 0
    kern = functools.partial(_attn_kernel, tq=tq, tk=tk, unroll=unroll, conv_rows=conv_rows)
    head = lambda n: pl.BlockSpec((1, n, HEAD_W), lambda bi, hi: (bi, 0, hi))
    head_t = lambda n: pl.BlockSpec((1, 1, V_ROWS, n), lambda bi, hi: (bi, hi, 0, 0))
    group = lambda n: pl.BlockSpec((n, HEAD_W), lambda bi, hi: (0, hi))
    n_heads = attn_w // HEAD_W
    steps = b * n_heads
    slab = lambda w: pl.BlockSpec((w.shape[0] // steps, w.shape[1]), lambda bi, hi: (bi * n_heads + hi, 0))
    return pl.pallas_call(
        kern,
        grid=(b, n_heads),
        in_specs=[pl.BlockSpec(lamv.shape, lambda bi, hi: (0, 0)),
                  pl.BlockSpec(subln_g_col.shape, lambda bi, hi: (0, 0)),
                  head(seq), head(seq), head(n_ctx), head_t(seq), head_t(n_ctx),
                  head(seq), group(w_dw.shape[0]), group(1)] + [slab(w) for w in cast_weights],
        out_specs=[head(seq), head(seq)] + [slab(w) for w in cast_weights],
        out_shape=[jax.ShapeDtypeStruct((b, seq, attn_w), BF16), jax.ShapeDtypeStruct(u.shape, F32)]
                  + [jax.ShapeDtypeStruct(w.shape, BF16) for w in cast_weights],
        scratch_shapes=[pltpu.VMEM((2, seq + n_ctx, tq), F32), pltpu.VMEM((seq + 2 * HALO, HEAD_W), F32)],
        compiler_params=_params(("arbitrary", "arbitrary")),
        name="attn",
    )(lamv, subln_g_col, q, k, kc, vt, vct, u, w_dw, b_dw, *cast_weights)


def _outproj_kernel(a_ref, c_ref, x_ref, mod_ref, w_ref, bo_ref, cg_ref, cb_ref, g_ref, b_ref, o_ref, z_ref,
                    *, chunk, rows):
    attn_w = a_ref.shape[1]
    gate = mod_ref[0, 2:3, :]
    for r0 in range(0, x_ref.shape[0], rows):
        rs = slice(r0, r0 + rows)
        a = a_ref[rs, :]
        cv = _normalize(c_ref[rs, :]) * cg_ref[...] + cb_ref[...]
        cv = (cv * _sigmoid(cv)).astype(BF16)
        for c0 in range(0, w_ref.shape[1], chunk):
            sl = slice(c0, c0 + chunk)
            y = (jnp.dot(a, w_ref[0:attn_w, sl], preferred_element_type=F32)
                 + jnp.dot(cv, w_ref[attn_w:, sl], preferred_element_type=F32) + bo_ref[:, sl])
            z_ref[rs, sl] = DEEPNORM_ALPHA * x_ref[rs, sl] + gate[:, sl] * y
        o_ref[rs, :] = _normalize(z_ref[rs, :]) * g_ref[...] + b_ref[...]


def _outproj(attn2d, conv2d, x2d, mod, w_out, b_out, conv_g, conv_b, g, b, seq, tm=512, chunk=512, rows=256):
    m, d = x2d.shape
    conv_w = conv2d.shape[1]
    tiles_per_seq = seq // tm
    kern = functools.partial(_outproj_kernel, chunk=chunk, rows=rows)
    row = lambda w: pl.BlockSpec((tm, w), lambda i: (i, 0))
    vec = lambda w: pl.BlockSpec((1, w), lambda i: (0, 0))
    return pl.pallas_call(
        kern,
        grid=(m // tm,),
        in_specs=[row(attn2d.shape[1]), row(conv_w), row(d),
                  pl.BlockSpec((1,) + mod.shape[1:], lambda i: (i // tiles_per_seq, 0, 0)),
                  _resident(w_out.shape), vec(d), vec(conv_w), vec(conv_w), vec(d), vec(d)],
        out_specs=row(d),
        out_shape=jax.ShapeDtypeStruct((m, d), F32),
        scratch_shapes=[pltpu.VMEM((tm, d), F32)],
        compiler_params=_params(("arbitrary",)),
        name="outproj",
    )(attn2d, conv2d, x2d, mod, w_out, b_out, conv_g, conv_b, g, b)


def _ffn_kernel(x_ref, mod_ref, w1_ref, b1_ref, w2_ref, b2_ref, g_ref, b_ref, o_ref, h_ref, *, rows):
    f = pl.program_id(1)
    last = pl.num_programs(1) - 1
    mod = mod_ref[0]

    def step(first, final):
        for r0 in range(0, x_ref.shape[0], rows):
            rs = slice(r0, r0 + rows)
            if first:
                h = (_normalize(x_ref[rs, :]) * (1.0 + mod[4:5]) + mod[3:4]).astype(BF16)
                h_ref[rs, :] = h
            else:
                h = h_ref[rs, :]
            t = jnp.maximum(jnp.dot(h, w1_ref[...], preferred_element_type=F32) + b1_ref[...], 0.0)
            y = jnp.dot((t * t).astype(BF16), w2_ref[...], preferred_element_type=F32)
            if not first:
                y = o_ref[rs, :] + y
            if final:
                z = DEEPNORM_ALPHA * x_ref[rs, :] + mod[5:6] * (y + b2_ref[...])
                y = _normalize(z) * g_ref[...] + b_ref[...]
            o_ref[rs, :] = y

    pl.when(f == 0)(functools.partial(step, True, False))
    pl.when((f > 0) & (f < last))(functools.partial(step, False, False))
    pl.when(f == last)(functools.partial(step, False, True))


def _ffn(x2d, mod, w1, b1, w2, b2, g, b, seq, tm=1024, tf=1024, rows=512):
    m, d = x2d.shape
    dff = w1.shape[1]
    tiles_per_seq = seq // tm
    assert dff // tf >= 2
    vec = pl.BlockSpec((1, d), lambda i, f: (0, 0))
    return pl.pallas_call(
        functools.partial(_ffn_kernel, rows=rows),
        grid=(m // tm, dff // tf),
        in_specs=[pl.BlockSpec((tm, d), lambda i, f: (i, 0)),
                  pl.BlockSpec((1,) + mod.shape[1:], lambda i, f: (i // tiles_per_seq, 0, 0)),
                  pl.BlockSpec((d, tf), lambda i, f: (0, f)),
                  pl.BlockSpec((1, tf), lambda i, f: (0, f)),
                  pl.BlockSpec((tf, d), lambda i, f: (f, 0)),
                  vec, vec, vec],
        out_specs=pl.BlockSpec((tm, d), lambda i, f: (i, 0)),
        out_shape=jax.ShapeDtypeStruct((m, d), F32),
        scratch_shapes=[pltpu.VMEM((tm, d), BF16)],
        compiler_params=_params(("arbitrary", "arbitrary")),
        name="ffn",
    )(x2d, mod, w1, b1, w2, b2, g, b)


def _rope_tables(seq):
    rows = seq // GRID_W
    axis_dim = HEAD_DIM // 2
    nf = axis_dim // 2
    inv_freq = ROPE_BASE ** (-jnp.arange(0, axis_dim, 2, dtype=F32) / axis_dim)
    ang_r = jnp.arange(rows, dtype=F32)[:, None] * inv_freq
    ang_c = jnp.arange(GRID_W, dtype=F32)[:, None] * inv_freq

    def expand(fn):
        r = jnp.broadcast_to(fn(ang_r)[:, None, :], (rows, GRID_W, nf)).reshape(seq, nf)
        c = jnp.broadcast_to(fn(ang_c)[None, :, :], (rows, GRID_W, nf)).reshape(seq, nf)
        return jnp.concatenate([r, r, c, c] * 2, axis=-1)

    cos, sin = expand(jnp.cos), expand(jnp.sin)
    first = (jnp.arange(HEAD_W) % (2 * nf)) < nf
    return cos, jnp.where(first, -sin, 0.0), jnp.where(first, 0.0, sin)


def kernel(x, c, ctx, c_ctx, w_ada, b_ada, w_in, b_glu, lambda_q1, lambda_k1, lambda_q2, lambda_k2, subln_g, w_dw, b_dw, conv_ln_g, conv_ln_b, w_out, b_out, ln1_g, ln1_b, w_ff1, b_ff1, w_ff2, b_ff2, ln2_g, ln2_b):
    assert w_ada.shape[0] == DEPTH
    bsz, seq, d = x.shape
    n_ctx = ctx.shape[1]
    attn_w = N_HEADS * HEAD_W
    conv_w = d - attn_w
    assert w_in.shape[2] == 3 * attn_w + 2 * conv_w and seq % GRID_W == 0

    ctx_row = bsz
    c8 = jnp.concatenate([c, c_ctx[None, :], jnp.zeros((8 - bsz - 1, d), F32)], axis=0)
    mod = _ada(c8, w_ada[0], b_ada).reshape(8, 6, d)

    w_in_b = w_in[0].astype(BF16)
    cos, sina, sinb = _rope_tables(seq)
    x2d = x.reshape(bsz * seq, d)
    q, k, vt, u = _inproj(x2d, mod, w_in_b, b_glu, cos, sina, sinb, seq, attn_w, conv_w)
    kc, vct = _inproj_ctx(ctx, mod, w_in_b, ctx_row, attn_w)

    lamv = jnp.concatenate([lambda_q1, lambda_k1, lambda_q2, lambda_k2], axis=0)
    w_dw_p = jnp.concatenate([w_dw[0], jnp.zeros((1, conv_w), F32)], axis=0)
    attn, conv, w_out_b, w_ff1_b, w_ff2_b = _attention(
        lamv, subln_g.reshape(HEAD_W, 1), q.reshape(bsz, seq, attn_w), k.reshape(bsz, seq, attn_w),
        kc, vt, vct, u.reshape(bsz, seq, conv_w), w_dw_p, b_dw, (w_out[0], w_ff1[0], w_ff2[0]))

    x1 = _outproj(attn.reshape(bsz * seq, attn_w), conv.reshape(bsz * seq, conv_w), x2d, mod,
                  w_out_b, b_out, conv_ln_g, conv_ln_b, ln1_g, ln1_b, seq)
    out = _ffn(x1, mod, w_ff1_b, b_ff1, w_ff2_b, b_ff2, ln2_g, ln2_b, seq)
    return out.reshape(bsz, seq, d)
```

```python
import functools
import math

import jax
import jax.numpy as jnp
from jax import lax
from jax.experimental import pallas as pl
from jax.experimental.pallas import tpu as pltpu

F32 = jnp.float32
BF16 = jnp.bfloat16

GRID_W = 64
N_HEADS = 8
HEAD_DIM = 64
HEAD_W = 2 * HEAD_DIM
V_PAD_ROWS = 16
V_ROWS = HEAD_W + V_PAD_ROWS
CONV_KERNEL = 31
CONV_PAD = CONV_KERNEL // 2
ROPE_BASE = 10000.0
LN_EPS = 1e-5
DEPTH = 1
DEEPNORM_ALPHA = (2.0 * DEPTH) ** 0.25
LAM_INIT = 0.8 - 0.6 * math.exp(-0.3 * 0)
Q_SCALE = HEAD_DIM ** -0.5 * math.log2(math.e)

V7X_VMEM_LIMIT = 60 * 1024 * 1024
HALO = 16


def _params(sem, vmem=V7X_VMEM_LIMIT):
    return pltpu.CompilerParams(dimension_semantics=sem, vmem_limit_bytes=vmem)


def _resident(shape):
    return pl.BlockSpec(shape, lambda *_: (0,) * len(shape), pipeline_mode=pl.Buffered(1))


def _normalize(x):
    mu = jnp.mean(x, axis=-1, keepdims=True)
    xc = x - mu
    var = jnp.mean(xc * xc, axis=-1, keepdims=True)
    return xc * lax.rsqrt(var + LN_EPS)


def _sigmoid(x):
    return 1.0 / (1.0 + jnp.exp(-x))


def _ones_row_block(n):
    row = lax.broadcasted_iota(jnp.int32, (V_PAD_ROWS, n), 0)
    return jnp.where(row == 0, 1.0, 0.0).astype(BF16)


def _ada_kernel(c_ref, w_ref, b_ref, o_ref):
    c = c_ref[...]
    s = (c * _sigmoid(c)).astype(BF16)
    o_ref[...] = jnp.dot(s, w_ref[...].astype(BF16), preferred_element_type=F32) + b_ref[...]


def _ada(c8, w_ada, b_ada, tn=1024):
    rows, d = c8.shape
    n = w_ada.shape[1]
    return pl.pallas_call(
        _ada_kernel,
        grid=(n // tn,),
        in_specs=[pl.BlockSpec((rows, d), lambda j: (0, 0)),
                  pl.BlockSpec((d, tn), lambda j: (0, j)),
                  pl.BlockSpec((1, tn), lambda j: (0, j))],
        out_specs=pl.BlockSpec((rows, tn), lambda j: (0, j)),
        out_shape=jax.ShapeDtypeStruct((rows, n), F32),
        compiler_params=_params(("arbitrary",)),
        name="ada",
    )(c8, w_ada, b_ada)


def _inproj_kernel(x_ref, mod_ref, w_ref, bglu_ref, cos_ref, sina_ref, sinb_ref,
                   q_ref, k_ref, vt_ref, u_ref, *, attn_w, conv_w, chunk, rows):
    mod = mod_ref[0]
    for r0 in range(0, x_ref.shape[0], rows):
        rs = slice(r0, r0 + rows)
        h = (_normalize(x_ref[rs, :]) * (1.0 + mod[1:2]) + mod[0:1]).astype(BF16)
        cos, sina, sinb = cos_ref[rs, :], sina_ref[rs, :], sinb_ref[rs, :]

        def proj(c0):
            return jnp.dot(h, w_ref[:, c0:c0 + chunk], preferred_element_type=F32)

        def rope(p):
            return p * cos + pltpu.roll(p, HEAD_W - 16, axis=1) * sina + pltpu.roll(p, 16, axis=1) * sinb

        for dst, base, scale in ((q_ref, 0, Q_SCALE), (k_ref, attn_w, None)):
            for j in range(attn_w // chunk):
                p = proj(base + j * chunk)
                for hh in range(chunk // HEAD_W):
                    r = rope(p[:, hh * HEAD_W:(hh + 1) * HEAD_W])
                    if scale is not None:
                        r = r * scale
                    c0 = j * chunk + hh * HEAD_W
                    dst[rs, c0:c0 + HEAD_W] = r.astype(BF16)
        for j in range(attn_w // chunk):
            vt = proj(2 * attn_w + j * chunk).T.astype(BF16)
            for hh in range(chunk // HEAD_W):
                head = j * (chunk // HEAD_W) + hh
                vt_ref[0, head, 0:HEAD_W, rs] = vt[hh * HEAD_W:(hh + 1) * HEAD_W]
                vt_ref[0, head, HEAD_W:, rs] = _ones_row_block(rows)
        for j in range(conv_w // chunk):
            a = proj(3 * attn_w + j * chunk) + bglu_ref[:, j * chunk:(j + 1) * chunk]
            g = proj(3 * attn_w + conv_w + j * chunk) + bglu_ref[:, conv_w + j * chunk:conv_w + (j + 1) * chunk]
            u_ref[rs, j * chunk:(j + 1) * chunk] = a * _sigmoid(g)


def _inproj(x2d, mod, w_in, b_glu, cos, sina, sinb, seq, attn_w, conv_w, tm=512, chunk=512, rows=256):
    m, d = x2d.shape
    tiles_per_seq = seq // tm
    kern = functools.partial(_inproj_kernel, attn_w=attn_w, conv_w=conv_w, chunk=chunk, rows=rows)
    tab = pl.BlockSpec((tm, HEAD_W), lambda i: (i % tiles_per_seq, 0))
    row = lambda w: pl.BlockSpec((tm, w), lambda i: (i, 0))
    return pl.pallas_call(
        kern,
        grid=(m // tm,),
        in_specs=[row(d),
                  pl.BlockSpec((1,) + mod.shape[1:], lambda i: (i // tiles_per_seq, 0, 0)),
                  _resident(w_in.shape), _resident(b_glu.shape), tab, tab, tab],
        out_specs=[row(attn_w), row(attn_w),
                   pl.BlockSpec((1, N_HEADS, V_ROWS, tm), lambda i: (i // tiles_per_seq, 0, 0, i % tiles_per_seq)),
                   row(conv_w)],
        out_shape=[jax.ShapeDtypeStruct((m, attn_w), BF16)] * 2
                  + [jax.ShapeDtypeStruct((m // seq, N_HEADS, V_ROWS, seq), BF16),
                     jax.ShapeDtypeStruct((m, conv_w), F32)],
        compiler_params=_params(("arbitrary",)),
        name="inproj",
    )(x2d, mod, w_in, b_glu, cos, sina, sinb)


def _inproj_ctx_kernel(x_ref, mod_ref, wk_ref, wv_ref, k_ref, vt_ref, *, chunk):
    mod = mod_ref[0]
    h = (_normalize(x_ref[0]) * (1.0 + mod[1:2]) + mod[0:1]).astype(BF16)
    for j in range(wk_ref.shape[1] // chunk):
        sl = slice(j * chunk, (j + 1) * chunk)
        k_ref[0, :, sl] = jnp.dot(h, wk_ref[:, sl], preferred_element_type=F32).astype(BF16)
        vt = jnp.dot(h, wv_ref[:, sl], preferred_element_type=F32).T.astype(BF16)
        for hh in range(chunk // HEAD_W):
            head = j * (chunk // HEAD_W) + hh
            vt_ref[0, head, 0:HEAD_W, :] = vt[hh * HEAD_W:(hh + 1) * HEAD_W]
            vt_ref[0, head, HEAD_W:, :] = _ones_row_block(h.shape[0])


def _inproj_ctx(ctx, mod, w_in, ctx_row, attn_w, chunk=512):
    bsz, n_ctx, d = ctx.shape
    kern = functools.partial(_inproj_ctx_kernel, chunk=chunk)
    return pl.pallas_call(
        kern,
        grid=(bsz,),
        in_specs=[pl.BlockSpec((1, n_ctx, d), lambda i: (i, 0, 0)),
                  pl.BlockSpec((1,) + mod.shape[1:], lambda i: (ctx_row, 0, 0)),
                  pl.BlockSpec((d, attn_w), lambda i: (0, 1)),
                  pl.BlockSpec((d, attn_w), lambda i: (0, 2))],
        out_specs=[pl.BlockSpec((1, n_ctx, attn_w), lambda i: (i, 0, 0)),
                   pl.BlockSpec((1, N_HEADS, V_ROWS, n_ctx), lambda i: (i, 0, 0, 0))],
        out_shape=[jax.ShapeDtypeStruct((bsz, n_ctx, attn_w), BF16),
                   jax.ShapeDtypeStruct((bsz, N_HEADS, V_ROWS, n_ctx), BF16)],
        compiler_params=_params(("arbitrary",)),
        name="inproj_ctx",
    )(ctx, mod, w_in, w_in)


def _attn_kernel(lamv_ref, g_ref, q_ref, k_ref, kc_ref, vt_ref, vct_ref, u_ref, wdw_ref, bdw_ref,
                 wo_ref, w1_ref, w2_ref, o_ref, cv_ref, wo16_ref, w116_ref, w216_ref, s_ref, ubuf_ref,
                 *, tq, tk, unroll, conv_rows):
    seq = k_ref.shape[1]
    n_ctx = kc_ref.shape[1]
    for src, dst in ((wo_ref, wo16_ref), (w1_ref, w116_ref), (w2_ref, w216_ref)):
        dst[...] = src[...].astype(BF16)

    ubuf_ref[0:HALO, :] = jnp.zeros((HALO, HEAD_W), F32)
    ubuf_ref[HALO + seq:, :] = jnp.zeros((HALO, HEAD_W), F32)
    ubuf_ref[HALO:HALO + seq, :] = u_ref[0]
    span = conv_rows + 8 * ((CONV_KERNEL - 1) // 8)
    win = span + 8

    def conv_block(r0, after):
        zero = lax.shift_right_logical(lax.shift_right_logical(after.astype(jnp.int32), 16), 16).astype(F32)
        xa = ubuf_ref[pl.ds(r0, win), :] + jnp.concatenate([zero] * (win // 8), axis=0)
        acc = jnp.zeros((conv_rows, HEAD_W), F32)
        for sub in range(8):
            shift = HALO - CONV_PAD + sub
            xs = pltpu.roll(xa, win - shift, axis=0) if shift % 8 else xa[shift:]
            for t in range(sub, CONV_KERNEL, 8):
                acc = acc + wdw_ref[t:t + 1, :] * xs[t - sub:t - sub + conv_rows]
        cv_ref[0, pl.ds(r0, conv_rows), :] = acc + bdw_ref[...]
    lv = lamv_ref[...]
    lam = (jnp.exp(jnp.sum(lv[0:1] * lv[1:2], axis=-1, keepdims=True))
           - jnp.exp(jnp.sum(lv[2:3] * lv[3:4], axis=-1, keepdims=True)) + LAM_INIT)
    lane = lax.broadcasted_iota(jnp.int32, (tq, HEAD_W), 1)
    tkc = min(tk, n_ctx)
    chunks = ([(k_ref, vt_ref, o, tk, o) for o in range(0, seq, tk)]
              + [(kc_ref, vct_ref, o, tkc, seq + o) for o in range(0, n_ctx, tkc)])
    nt = (((1,), (1,)), ((), ()))

    def fold8(acc, x, op):
        parts = [x[r:r + 8] for r in range(0, x.shape[0], 8)]
        while len(parts) > 1:
            parts = [op(parts[a], parts[a + 1]) for a in range(0, len(parts), 2)]
        return op(acc, parts[0])

    def q_half(tile_idx, c):
        q = q_ref[0, pl.ds(pl.multiple_of(tile_idx * tq, tq), tq), :]
        return jnp.where((lane >= c * HEAD_DIM) & (lane < (c + 1) * HEAD_DIM), q, jnp.zeros_like(q))

    def stage(c_out, m_out, c_in, q_in, conv_sites=()):
        m8 = jnp.full((8, tq), -jnp.inf, F32)
        acc = jnp.zeros((V_ROWS, tq), F32)
        for j, (kr, vr, off, n, pos) in enumerate(chunks):
            if c_out is not None:
                p = jnp.exp2(s_ref[c_out, pos:pos + n, :] - m_out).astype(BF16)
                acc = acc + jnp.dot(vr[0, 0, :, off:off + n], p, preferred_element_type=F32)
            st = lax.dot_general(kr[0, off:off + n, :], q_in, nt, preferred_element_type=F32)
            s_ref[c_in, pos:pos + n, :] = st
            m8 = fold8(m8, st, jnp.maximum)
            for site, r0 in conv_sites:
                if site == j:
                    conv_block(r0, st[0:8, 0:HEAD_W])
        out = None if c_out is None else acc[0:HEAD_W] / acc[HEAD_W:HEAD_W + 1]
        return out, jnp.max(m8, axis=0, keepdims=True)

    n_tiles = seq // tq
    n_blocks = tq // conv_rows
    per_stage = n_blocks // 2
    every = len(chunks) // per_stage
    sites = [every * b + every // 2 for b in range(per_stage)]

    def tile(i, m0):
        starts = [pl.multiple_of(i * tq + b * conv_rows, conv_rows) for b in range(n_blocks)]
        o0, m1 = stage(0, m0, 1, q_half(i, 1), list(zip(sites, starts[:per_stage])))
        o1, m0_next = stage(1, m1, 0, q_half(jnp.minimum(i + 1, n_tiles - 1), 0),
                            list(zip(sites, starts[per_stage:])))
        ot = o0 - lam * o1
        ot = ot * lax.rsqrt(jnp.mean(ot * ot, axis=0, keepdims=True) + LN_EPS) * g_ref[...] * (1.0 - LAM_INIT)
        o_ref[0, pl.ds(pl.multiple_of(i * tq, tq), tq), :] = ot.T.astype(BF16)
        return m0_next

    _, m_first = stage(None, None, 0, q_half(0, 0))
    lax.fori_loop(0, n_tiles, tile, m_first, unroll=unroll)


def _attention(lamv, subln_g_col, q, k, kc, vt, vct, u, w_dw, b_dw, cast_weights,
               tq=512, tk=256, unroll=1, conv_rows=32):
    assert len(cast_weights) == 3
    b, seq, attn_w = q.shape
    n_ctx = kc.shape[1]
    assert u.shape[2] == attn_w and w_dw.shape[0] % 8 == 0
    kern = functools.partial(_attn_kernel, tq=tq, tk=tk, unroll=unroll, conv_rows=conv_rows)
    head = lambda n: pl.BlockSpec((1, n, HEAD_W), lambda bi, hi: (bi, 0, hi))
    head_t = lambda n: pl.BlockSpec((1, 1, V_ROWS, n), lambda bi, hi: (bi, hi, 0, 0))
    group = lambda n: pl.BlockSpec((n, HEAD_W), lambda bi, hi: (0, hi))
    n_heads = attn_w // HEAD_W
    steps = b * n_heads
    slab = lambda w: pl.BlockSpec((w.shape[0] // steps, w.shape[1]), lambda bi, hi: (bi * n_heads + hi, 0))
    return pl.pallas_call(
        kern,
        grid=(b, n_heads),
        in_specs=[pl.BlockSpec(lamv.shape, lambda bi, hi: (0, 0)),
                  pl.BlockSpec(subln_g_col.shape, lambda bi, hi: (0, 0)),
                  head(seq), head(seq), head(n_ctx), head_t(seq), head_t(n_ctx),
                  head(seq), group(w_dw.shape[0]), group(1)] + [slab(w) for w in cast_weights],
        out_specs=[head(seq), head(seq)] + [slab(w) for w in cast_weights],
        out_shape=[jax.ShapeDtypeStruct((b, seq, attn_w), BF16), jax.ShapeDtypeStruct(u.shape, F32)]
                  + [jax.ShapeDtypeStruct(w.shape, BF16) for w in cast_weights],
        scratch_shapes=[pltpu.VMEM((2, seq + n_ctx, tq), F32), pltpu.VMEM((seq + 2 * HALO, HEAD_W), F32)],
        compiler_params=_params(("arbitrary", "arbitrary")),
        name="attn",
    )(lamv, subln_g_col, q, k, kc, vt, vct, u, w_dw, b_dw, *cast_weights)


def _outproj_kernel(a_ref, c_ref, x_ref, mod_ref, w_ref, bo_ref, cg_ref, cb_ref, g_ref, b_ref, o_ref, z_ref,
                    *, chunk, rows):
    attn_w = a_ref.shape[1]
    gate = mod_ref[0, 2:3, :]
    for r0 in range(0, x_ref.shape[0], rows):
        rs = slice(r0, r0 + rows)
        a = a_ref[rs, :]
        cv = _normalize(c_ref[rs, :]) * cg_ref[...] + cb_ref[...]
        cv = (cv * _sigmoid(cv)).astype(BF16)
        for c0 in range(0, w_ref.shape[1], chunk):
            sl = slice(c0, c0 + chunk)
            y = (jnp.dot(a, w_ref[0:attn_w, sl], preferred_element_type=F32)
                 + jnp.dot(cv, w_ref[attn_w:, sl], preferred_element_type=F32) + bo_ref[:, sl])
            z_ref[rs, sl] = DEEPNORM_ALPHA * x_ref[rs, sl] + gate[:, sl] * y
        o_ref[rs, :] = _normalize(z_ref[rs, :]) * g_ref[...] + b_ref[...]


def _outproj(attn2d, conv2d, x2d, mod, w_out, b_out, conv_g, conv_b, g, b, seq, tm=512, chunk=512, rows=256):
    m, d = x2d.shape
    conv_w = conv2d.shape[1]
    tiles_per_seq = seq // tm
    kern = functools.partial(_outproj_kernel, chunk=chunk, rows=rows)
    row = lambda w: pl.BlockSpec((tm, w), lambda i: (i, 0))
    vec = lambda w: pl.BlockSpec((1, w), lambda i: (0, 0))
    return pl.pallas_call(
        kern,
        grid=(m // tm,),
        in_specs=[row(attn2d.shape[1]), row(conv_w), row(d),
                  pl.BlockSpec((1,) + mod.shape[1:], lambda i: (i // tiles_per_seq, 0, 0)),
                  _resident(w_out.shape), vec(d), vec(conv_w), vec(conv_w), vec(d), vec(d)],
        out_specs=row(d),
        out_shape=jax.ShapeDtypeStruct((m, d), F32),
        scratch_shapes=[pltpu.VMEM((tm, d), F32)],
        compiler_params=_params(("arbitrary",)),
        name="outproj",
    )(attn2d, conv2d, x2d, mod, w_out, b_out, conv_g, conv_b, g, b)


def _ffn_kernel(x_ref, mod_ref, w1_ref, b1_ref, w2_ref, b2_ref, g_ref, b_ref, o_ref, h_ref, *, rows):
    f = pl.program_id(1)
    last = pl.num_programs(1) - 1
    mod = mod_ref[0]

    def step(first, final):
        for r0 in range(0, x_ref.shape[0], rows):
            rs = slice(r0, r0 + rows)
            if first:
                h = (_normalize(x_ref[rs, :]) * (1.0 + mod[4:5]) + mod[3:4]).astype(BF16)
                h_ref[rs, :] = h
            else:
                h = h_ref[rs, :]
            t = jnp.maximum(jnp.dot(h, w1_ref[...], preferred_element_type=F32) + b1_ref[...], 0.0)
            y = jnp.dot((t * t).astype(BF16), w2_ref[...], preferred_element_type=F32)
            if not first:
                y = o_ref[rs, :] + y
            if final:
                z = DEEPNORM_ALPHA * x_ref[rs, :] + mod[5:6] * (y + b2_ref[...])
                y = _normalize(z) * g_ref[...] + b_ref[...]
            o_ref[rs, :] = y

    pl.when(f == 0)(functools.partial(step, True, False))
    pl.when((f > 0) & (f < last))(functools.partial(step, False, False))
    pl.when(f == last)(functools.partial(step, False, True))


def _ffn(x2d, mod, w1, b1, w2, b2, g, b, seq, tm=1024, tf=1024, rows=512):
    m, d = x2d.shape
    dff = w1.shape[1]
    tiles_per_seq = seq // tm
    assert dff // tf >= 2
    vec = pl.BlockSpec((1, d), lambda i, f: (0, 0))
    return pl.pallas_call(
        functools.partial(_ffn_kernel, rows=rows),
        grid=(m // tm, dff // tf),
        in_specs=[pl.BlockSpec((tm, d), lambda i, f: (i, 0)),
                  pl.BlockSpec((1,) + mod.shape[1:], lambda i, f: (i // tiles_per_seq, 0, 0)),
                  pl.BlockSpec((d, tf), lambda i, f: (0, f)),
                  pl.BlockSpec((1, tf), lambda i, f: (0, f)),
                  pl.BlockSpec((tf, d), lambda i, f: (f, 0)),
                  vec, vec, vec],
        out_specs=pl.BlockSpec((tm, d), lambda i, f: (i, 0)),
        out_shape=jax.ShapeDtypeStruct((m, d), F32),
        scratch_shapes=[pltpu.VMEM((tm, d), BF16)],
        compiler_params=_params(("arbitrary", "arbitrary")),
        name="ffn",
    )(x2d, mod, w1, b1, w2, b2, g, b)


def _rope_tables(seq):
    rows = seq // GRID_W
    axis_dim = HEAD_DIM // 2
    nf = axis_dim // 2
    inv_freq = ROPE_BASE ** (-jnp.arange(0, axis_dim, 2, dtype=F32) / axis_dim)
    lane = jnp.arange(HEAD_W)
    inv_lane = jnp.tile(inv_freq, HEAD_W // nf)
    ang_r = jnp.arange(rows, dtype=F32)[:, None] * inv_lane
    ang_c = jnp.arange(GRID_W, dtype=F32)[:, None] * inv_lane
    row_lane = (lane % HEAD_DIM) < axis_dim

    def expand(fn):
        return jnp.where(row_lane, fn(ang_r)[:, None, :], fn(ang_c)[None, :, :]).reshape(seq, HEAD_W)

    cos, sin = expand(jnp.cos), expand(jnp.sin)
    first = (lane % (2 * nf)) < nf
    return cos, jnp.where(first, -sin, 0.0), jnp.where(first, 0.0, sin)


def kernel(x, c, ctx, c_ctx, w_ada, b_ada, w_in, b_glu, lambda_q1, lambda_k1, lambda_q2, lambda_k2, subln_g, w_dw, b_dw, conv_ln_g, conv_ln_b, w_out, b_out, ln1_g, ln1_b, w_ff1, b_ff1, w_ff2, b_ff2, ln2_g, ln2_b):
    assert w_ada.shape[0] == DEPTH
    bsz, seq, d = x.shape
    n_ctx = ctx.shape[1]
    attn_w = N_HEADS * HEAD_W
    conv_w = d - attn_w
    assert w_in.shape[2] == 3 * attn_w + 2 * conv_w and seq % GRID_W == 0

    ctx_row = bsz
    c8 = jnp.concatenate([c, c_ctx[None, :], jnp.zeros((8 - bsz - 1, d), F32)], axis=0)
    mod = _ada(c8, w_ada[0], b_ada).reshape(8, 6, d)

    w_in_b = w_in[0].astype(BF16)
    cos, sina, sinb = _rope_tables(seq)
    x2d = x.reshape(bsz * seq, d)
    q, k, vt, u = _inproj(x2d, mod, w_in_b, b_glu, cos, sina, sinb, seq, attn_w, conv_w)
    kc, vct = _inproj_ctx(ctx, mod, w_in_b, ctx_row, attn_w)

    lamv = jnp.concatenate([lambda_q1, lambda_k1, lambda_q2, lambda_k2], axis=0)
    w_dw_p = jnp.concatenate([w_dw[0], jnp.zeros((1, conv_w), F32)], axis=0)
    attn, conv, w_out_b, w_ff1_b, w_ff2_b = _attention(
        lamv, subln_g.reshape(HEAD_W, 1), q.reshape(bsz, seq, attn_w), k.reshape(bsz, seq, attn_w),
        kc, vt, vct, u.reshape(bsz, seq, conv_w), w_dw_p, b_dw, (w_out[0], w_ff1[0], w_ff2[0]))

    x1 = _outproj(attn.reshape(bsz * seq, attn_w), conv.reshape(bsz * seq, conv_w), x2d, mod,
                  w_out_b, b_out, conv_ln_g, conv_ln_b, ln1_g, ln1_b, seq)
    out = _ffn(x1, mod, w_ff1_b, b_ff1, w_ff2_b, b_ff2, ln2_g, ln2_b, seq)
    return out.reshape(bsz, seq, d)
```

```python
import functools
import math

import jax
import jax.numpy as jnp
from jax import lax
from jax.experimental import pallas as pl
from jax.experimental.pallas import tpu as pltpu

F32 = jnp.float32
BF16 = jnp.bfloat16

GRID_W = 64
N_HEADS = 8
HEAD_DIM = 64
HEAD_W = 2 * HEAD_DIM
V_PAD_ROWS = 16
V_ROWS = HEAD_W + V_PAD_ROWS
CONV_KERNEL = 31
CONV_PAD = CONV_KERNEL // 2
ROPE_BASE = 10000.0
LN_EPS = 1e-5
DEPTH = 1
DEEPNORM_ALPHA = (2.0 * DEPTH) ** 0.25
LAM_INIT = 0.8 - 0.6 * math.exp(-0.3 * 0)
Q_SCALE = HEAD_DIM ** -0.5 * math.log2(math.e)

V7X_VMEM_LIMIT = 60 * 1024 * 1024
HALO = 16


def _params(sem, vmem=V7X_VMEM_LIMIT):
    return pltpu.CompilerParams(dimension_semantics=sem, vmem_limit_bytes=vmem)


def _resident(shape):
    return pl.BlockSpec(shape, lambda *_: (0,) * len(shape), pipeline_mode=pl.Buffered(1))


def _normalize(x):
    mu = jnp.mean(x, axis=-1, keepdims=True)
    xc = x - mu
    var = jnp.mean(xc * xc, axis=-1, keepdims=True)
    return xc * lax.rsqrt(var + LN_EPS)


def _sigmoid(x):
    return 1.0 / (1.0 + jnp.exp(-x))


def _ones_row_block(n):
    row = lax.broadcasted_iota(jnp.int32, (V_PAD_ROWS, n), 0)
    return jnp.where(row == 0, 1.0, 0.0).astype(BF16)


def _ada_kernel(c_ref, w_ref, b_ref, o_ref):
    c = c_ref[...]
    s = (c * _sigmoid(c)).astype(BF16)
    o_ref[...] = jnp.dot(s, w_ref[...].astype(BF16), preferred_element_type=F32) + b_ref[...]


def _ada(c8, w_ada, b_ada, tn=1024):
    rows, d = c8.shape
    n = w_ada.shape[1]
    return pl.pallas_call(
        _ada_kernel,
        grid=(n // tn,),
        in_specs=[pl.BlockSpec((rows, d), lambda j: (0, 0)),
                  pl.BlockSpec((d, tn), lambda j: (0, j)),
                  pl.BlockSpec((1, tn), lambda j: (0, j))],
        out_specs=pl.BlockSpec((rows, tn), lambda j: (0, j)),
        out_shape=jax.ShapeDtypeStruct((rows, n), F32),
        compiler_params=_params(("arbitrary",)),
        name="ada",
    )(c8, w_ada, b_ada)


def _inproj_kernel(x_ref, mod_ref, w_ref, bglu_ref, cos_ref, sina_ref, sinb_ref,
                   q_ref, k_ref, vt_ref, u_ref, *, attn_w, conv_w, chunk, rows):
    mod = mod_ref[0]
    for r0 in range(0, x_ref.shape[0], rows):
        rs = slice(r0, r0 + rows)
        h = (_normalize(x_ref[rs, :]) * (1.0 + mod[1:2]) + mod[0:1]).astype(BF16)
        cos, sina, sinb = cos_ref[rs, :], sina_ref[rs, :], sinb_ref[rs, :]

        def proj(c0):
            return jnp.dot(h, w_ref[:, c0:c0 + chunk], preferred_element_type=F32)

        def rope(p):
            return p * cos + pltpu.roll(p, HEAD_W - 16, axis=1) * sina + pltpu.roll(p, 16, axis=1) * sinb

        for dst, base, scale in ((q_ref, 0, Q_SCALE), (k_ref, attn_w, None)):
            for j in range(attn_w // chunk):
                p = proj(base + j * chunk)
                for hh in range(chunk // HEAD_W):
                    r = rope(p[:, hh * HEAD_W:(hh + 1) * HEAD_W])
                    if scale is not None:
                        r = r * scale
                    c0 = j * chunk + hh * HEAD_W
                    dst[rs, c0:c0 + HEAD_W] = r.astype(BF16)
        for j in range(attn_w // chunk):
            vt = proj(2 * attn_w + j * chunk).T.astype(BF16)
            for hh in range(chunk // HEAD_W):
                head = j * (chunk // HEAD_W) + hh
                vt_ref[0, head, 0:HEAD_W, rs] = vt[hh * HEAD_W:(hh + 1) * HEAD_W]
                vt_ref[0, head, HEAD_W:, rs] = _ones_row_block(rows)
        for j in range(conv_w // chunk):
            a = proj(3 * attn_w + j * chunk) + bglu_ref[:, j * chunk:(j + 1) * chunk]
            g = proj(3 * attn_w + conv_w + j * chunk) + bglu_ref[:, conv_w + j * chunk:conv_w + (j + 1) * chunk]
            u_ref[rs, j * chunk:(j + 1) * chunk] = a * _sigmoid(g)


def _inproj(x2d, mod, w_in, b_glu, cos, sina, sinb, seq, attn_w, conv_w, tm=512, chunk=512, rows=256):
    m, d = x2d.shape
    tiles_per_seq = seq // tm
    kern = functools.partial(_inproj_kernel, attn_w=attn_w, conv_w=conv_w, chunk=chunk, rows=rows)
    tab = pl.BlockSpec((tm, HEAD_W), lambda i: (i % tiles_per_seq, 0))
    row = lambda w: pl.BlockSpec((tm, w), lambda i: (i, 0))
    return pl.pallas_call(
        kern,
        grid=(m // tm,),
        in_specs=[row(d),
                  pl.BlockSpec((1,) + mod.shape[1:], lambda i: (i // tiles_per_seq, 0, 0)),
                  _resident(w_in.shape), _resident(b_glu.shape), tab, tab, tab],
        out_specs=[row(attn_w), row(attn_w),
                   pl.BlockSpec((1, N_HEADS, V_ROWS, tm), lambda i: (i // tiles_per_seq, 0, 0, i % tiles_per_seq)),
                   row(conv_w)],
        out_shape=[jax.ShapeDtypeStruct((m, attn_w), BF16)] * 2
                  + [jax.ShapeDtypeStruct((m // seq, N_HEADS, V_ROWS, seq), BF16),
                     jax.ShapeDtypeStruct((m, conv_w), F32)],
        compiler_params=_params(("arbitrary",)),
        name="inproj",
    )(x2d, mod, w_in, b_glu, cos, sina, sinb)


def _inproj_ctx_kernel(x_ref, mod_ref, wk_ref, wv_ref, k_ref, vt_ref, *, chunk):
    mod = mod_ref[0]
    h = (_normalize(x_ref[0]) * (1.0 + mod[1:2]) + mod[0:1]).astype(BF16)
    for j in range(wk_ref.shape[1] // chunk):
        sl = slice(j * chunk, (j + 1) * chunk)
        k_ref[0, :, sl] = jnp.dot(h, wk_ref[:, sl], preferred_element_type=F32).astype(BF16)
        vt = jnp.dot(h, wv_ref[:, sl], preferred_element_type=F32).T.astype(BF16)
        for hh in range(chunk // HEAD_W):
            head = j * (chunk // HEAD_W) + hh
            vt_ref[0, head, 0:HEAD_W, :] = vt[hh * HEAD_W:(hh + 1) * HEAD_W]
            vt_ref[0, head, HEAD_W:, :] = _ones_row_block(h.shape[0])


def _inproj_ctx(ctx, mod, w_in, ctx_row, attn_w, chunk=512):
    bsz, n_ctx, d = ctx.shape
    kern = functools.partial(_inproj_ctx_kernel, chunk=chunk)
    return pl.pallas_call(
        kern,
        grid=(bsz,),
        in_specs=[pl.BlockSpec((1, n_ctx, d), lambda i: (i, 0, 0)),
                  pl.BlockSpec((1,) + mod.shape[1:], lambda i: (ctx_row, 0, 0)),
                  pl.BlockSpec((d, attn_w), lambda i: (0, 1)),
                  pl.BlockSpec((d, attn_w), lambda i: (0, 2))],
        out_specs=[pl.BlockSpec((1, n_ctx, attn_w), lambda i: (i, 0, 0)),
                   pl.BlockSpec((1, N_HEADS, V_ROWS, n_ctx), lambda i: (i, 0, 0, 0))],
        out_shape=[jax.ShapeDtypeStruct((bsz, n_ctx, attn_w), BF16),
                   jax.ShapeDtypeStruct((bsz, N_HEADS, V_ROWS, n_ctx), BF16)],
        compiler_params=_params(("arbitrary",)),
        name="inproj_ctx",
    )(ctx, mod, w_in, w_in)


def _attn_kernel(lamv_ref, g_ref, q_ref, k_ref, kc_ref, vt_ref, vct_ref, u_ref, wdw_ref, bdw_ref,
                 wo_ref, w1_ref, w2_ref, o_ref, cv_ref, wo16_ref, w116_ref, w216_ref, s_ref, ubuf_ref,
                 *, tq, tk, unroll, conv_rows):
    seq = k_ref.shape[1]
    n_ctx = kc_ref.shape[1]
    for src, dst in ((wo_ref, wo16_ref), (w1_ref, w116_ref), (w2_ref, w216_ref)):
        dst[...] = src[...].astype(BF16)

    ubuf_ref[0:HALO, :] = jnp.zeros((HALO, HEAD_W), F32)
    ubuf_ref[HALO + seq:, :] = jnp.zeros((HALO, HEAD_W), F32)
    ubuf_ref[HALO:HALO + seq, :] = u_ref[0]
    span = conv_rows + 8 * ((CONV_KERNEL - 1) // 8)
    win = span + 8

    def conv_block(r0, after):
        zero = lax.shift_right_logical(lax.shift_right_logical(after.astype(jnp.int32), 16), 16).astype(F32)
        xa = ubuf_ref[pl.ds(r0, win), :] + jnp.concatenate([zero] * (win // 8), axis=0)
        acc = jnp.zeros((conv_rows, HEAD_W), F32)
        for sub in range(8):
            shift = HALO - CONV_PAD + sub
            xs = pltpu.roll(xa, win - shift, axis=0) if shift % 8 else xa[shift:]
            for t in range(sub, CONV_KERNEL, 8):
                acc = acc + wdw_ref[t:t + 1, :] * xs[t - sub:t - sub + conv_rows]
        cv_ref[0, pl.ds(r0, conv_rows), :] = acc + bdw_ref[...]
    lv = lamv_ref[...]
    lam = (jnp.exp(jnp.sum(lv[0:1] * lv[1:2], axis=-1, keepdims=True))
           - jnp.exp(jnp.sum(lv[2:3] * lv[3:4], axis=-1, keepdims=True)) + LAM_INIT)
    lane = lax.broadcasted_iota(jnp.int32, (tq, HEAD_W), 1)
    tkc = min(tk, n_ctx)
    chunks = ([(k_ref, vt_ref, o, tk, o) for o in range(0, seq, tk)]
              + [(kc_ref, vct_ref, o, tkc, seq + o) for o in range(0, n_ctx, tkc)])
    nt = (((1,), (1,)), ((), ()))

    def fold8(acc, x, op):
        parts = [x[r:r + 8] for r in range(0, x.shape[0], 8)]
        while len(parts) > 1:
            parts = [op(parts[a], parts[a + 1]) for a in range(0, len(parts), 2)]
        return op(acc, parts[0])

    def q_half(tile_idx, c):
        q = q_ref[0, pl.ds(pl.multiple_of(tile_idx * tq, tq), tq), :]
        return jnp.where((lane >= c * HEAD_DIM) & (lane < (c + 1) * HEAD_DIM), q, jnp.zeros_like(q))

    def stage(c_out, m_out, c_in, q_in, conv_sites=()):
        m8 = jnp.full((8, tq), -jnp.inf, F32)
        acc = jnp.zeros((V_ROWS, tq), F32)
        for j, (kr, vr, off, n, pos) in enumerate(chunks):
            if c_out is not None:
                p = jnp.exp2(s_ref[c_out, pos:pos + n, :] - m_out).astype(BF16)
                acc = acc + jnp.dot(vr[0, 0, :, off:off + n], p, preferred_element_type=F32)
            st = lax.dot_general(kr[0, off:off + n, :], q_in, nt, preferred_element_type=F32)
            s_ref[c_in, pos:pos + n, :] = st
            m8 = fold8(m8, st, jnp.maximum)
            for site, r0 in conv_sites:
                if site == j:
                    conv_block(r0, st[0:8, 0:HEAD_W])
        out = None if c_out is None else acc[0:HEAD_W] / acc[HEAD_W:HEAD_W + 1]
        return out, jnp.max(m8, axis=0, keepdims=True)

    n_tiles = seq // tq
    n_blocks = tq // conv_rows
    per_stage = n_blocks // 2
    every = len(chunks) // per_stage
    sites = [every * b + every // 2 for b in range(per_stage)]

    def tile(i, m0):
        starts = [pl.multiple_of(i * tq + b * conv_rows, conv_rows) for b in range(n_blocks)]
        o0, m1 = stage(0, m0, 1, q_half(i, 1), list(zip(sites, starts[:per_stage])))
        o1, m0_next = stage(1, m1, 0, q_half(jnp.minimum(i + 1, n_tiles - 1), 0),
                            list(zip(sites, starts[per_stage:])))
        ot = o0 - lam * o1
        ot = ot * lax.rsqrt(jnp.mean(ot * ot, axis=0, keepdims=True) + LN_EPS) * g_ref[...] * (1.0 - LAM_INIT)
        o_ref[0, pl.ds(pl.multiple_of(i * tq, tq), tq), :] = ot.T.astype(BF16)
        return m0_next

    _, m_first = stage(None, None, 0, q_half(0, 0))
    lax.fori_loop(0, n_tiles, tile, m_first, unroll=unroll)


def _attention(lamv, subln_g_col, q, k, kc, vt, vct, u, w_dw, b_dw, cast_weights,
               tq=256, tk=256, unroll=1, conv_rows=32):
    assert len(cast_weights) == 3
    b, seq, attn_w = q.shape
    n_ctx = kc.shape[1]
    assert u.shape[2] == attn_w and w_dw.shape[0] % 8 == 0
    kern = functools.partial(_attn_kernel, tq=tq, tk=tk, unroll=unroll, conv_rows=conv_rows)
    head = lambda n: pl.BlockSpec((1, n, HEAD_W), lambda bi, hi: (bi, 0, hi))
    head_t = lambda n: pl.BlockSpec((1, 1, V_ROWS, n), lambda bi, hi: (bi, hi, 0, 0))
    group = lambda n: pl.BlockSpec((n, HEAD_W), lambda bi, hi: (0, hi))
    n_heads = attn_w // HEAD_W
    steps = b * n_heads
    slab = lambda w: pl.BlockSpec((w.shape[0] // steps, w.shape[1]), lambda bi, hi: (bi * n_heads + hi, 0))
    return pl.pallas_call(
        kern,
        grid=(b, n_heads),
        in_specs=[pl.BlockSpec(lamv.shape, lambda bi, hi: (0, 0)),
                  pl.BlockSpec(subln_g_col.shape, lambda bi, hi: (0, 0)),
                  head(seq), head(seq), head(n_ctx), head_t(seq), head_t(n_ctx),
                  head(seq), group(w_dw.shape[0]), group(1)] + [slab(w) for w in cast_weights],
        out_specs=[head(seq), head(seq)] + [slab(w) for w in cast_weights],
        out_shape=[jax.ShapeDtypeStruct((b, seq, attn_w), BF16), jax.ShapeDtypeStruct(u.shape, F32)]
                  + [jax.ShapeDtypeStruct(w.shape, BF16) for w in cast_weights],
        scratch_shapes=[pltpu.VMEM((2, seq + n_ctx, tq), F32), pltpu.VMEM((seq + 2 * HALO, HEAD_W), F32)],
        compiler_params=_params(("arbitrary", "arbitrary")),
        name="attn",
    )(lamv, subln_g_col, q, k, kc, vt, vct, u, w_dw, b_dw, *cast_weights)


def _outproj_kernel(a_ref, c_ref, x_ref, mod_ref, w_ref, bo_ref, cg_ref, cb_ref, g_ref, b_ref, o_ref, z_ref,
                    *, chunk, rows):
    attn_w = a_ref.shape[1]
    gate = mod_ref[0, 2:3, :]
    for r0 in range(0, x_ref.shape[0], rows):
        rs = slice(r0, r0 + rows)
        a = a_ref[rs, :]
        cv = _normalize(c_ref[rs, :]) * cg_ref[...] + cb_ref[...]
        cv = (cv * _sigmoid(cv)).astype(BF16)
        for c0 in range(0, w_ref.shape[1], chunk):
            sl = slice(c0, c0 + chunk)
            y = (jnp.dot(a, w_ref[0:attn_w, sl], preferred_element_type=F32)
                 + jnp.dot(cv, w_ref[attn_w:, sl], preferred_element_type=F32) + bo_ref[:, sl])
            z_ref[rs, sl] = DEEPNORM_ALPHA * x_ref[rs, sl] + gate[:, sl] * y
        o_ref[rs, :] = _normalize(z_ref[rs, :]) * g_ref[...] + b_ref[...]


def _outproj(attn2d, conv2d, x2d, mod, w_out, b_out, conv_g, conv_b, g, b, seq, tm=512, chunk=512, rows=256):
    m, d = x2d.shape
    conv_w = conv2d.shape[1]
    tiles_per_seq = seq // tm
    kern = functools.partial(_outproj_kernel, chunk=chunk, rows=rows)
    row = lambda w: pl.BlockSpec((tm, w), lambda i: (i, 0))
    vec = lambda w: pl.BlockSpec((1, w), lambda i: (0, 0))
    return pl.pallas_call(
        kern,
        grid=(m // tm,),
        in_specs=[row(attn2d.shape[1]), row(conv_w), row(d),
                  pl.BlockSpec((1,) + mod.shape[1:], lambda i: (i // tiles_per_seq, 0, 0)),
                  _resident(w_out.shape), vec(d), vec(conv_w), vec(conv_w), vec(d), vec(d)],
        out_specs=row(d),
        out_shape=jax.ShapeDtypeStruct((m, d), F32),
        scratch_shapes=[pltpu.VMEM((tm, d), F32)],
        compiler_params=_params(("arbitrary",)),
        name="outproj",
    )(attn2d, conv2d, x2d, mod, w_out, b_out, conv_g, conv_b, g, b)


def _ffn_kernel(x_ref, mod_ref, w1_ref, b1_ref, w2_ref, b2_ref, g_ref, b_ref, o_ref, h_ref, *, rows):
    f = pl.program_id(1)
    last = pl.num_programs(1) - 1
    mod = mod_ref[0]

    def step(first, final):
        for r0 in range(0, x_ref.shape[0], rows):
            rs = slice(r0, r0 + rows)
            if first:
                h = (_normalize(x_ref[rs, :]) * (1.0 + mod[4:5]) + mod[3:4]).astype(BF16)
                h_ref[rs, :] = h
            else:
                h = h_ref[rs, :]
            t = jnp.maximum(jnp.dot(h, w1_ref[...], preferred_element_type=F32) + b1_ref[...], 0.0)
            y = jnp.dot((t * t).astype(BF16), w2_ref[...], preferred_element_type=F32)
            if not first:
                y = o_ref[rs, :] + y
            if final:
                z = DEEPNORM_ALPHA * x_ref[rs, :] + mod[5:6] * (y + b2_ref[...])
                y = _normalize(z) * g_ref[...] + b_ref[...]
            o_ref[rs, :] = y

    pl.when(f == 0)(functools.partial(step, True, False))
    pl.when((f > 0) & (f < last))(functools.partial(step, False, False))
    pl.when(f == last)(functools.partial(step, False, True))


def _ffn(x2d, mod, w1, b1, w2, b2, g, b, seq, tm=1024, tf=1024, rows=512):
    m, d = x2d.shape
    dff = w1.shape[1]
    tiles_per_seq = seq // tm
    assert dff // tf >= 2
    vec = pl.BlockSpec((1, d), lambda i, f: (0, 0))
    return pl.pallas_call(
        functools.partial(_ffn_kernel, rows=rows),
        grid=(m // tm, dff // tf),
        in_specs=[pl.BlockSpec((tm, d), lambda i, f: (i, 0)),
                  pl.BlockSpec((1,) + mod.shape[1:], lambda i, f: (i // tiles_per_seq, 0, 0)),
                  pl.BlockSpec((d, tf), lambda i, f: (0, f)),
                  pl.BlockSpec((1, tf), lambda i, f: (0, f)),
                  pl.BlockSpec((tf, d), lambda i, f: (f, 0)),
                  vec, vec, vec],
        out_specs=pl.BlockSpec((tm, d), lambda i, f: (i, 0)),
        out_shape=jax.ShapeDtypeStruct((m, d), F32),
        scratch_shapes=[pltpu.VMEM((tm, d), BF16)],
        compiler_params=_params(("arbitrary", "arbitrary")),
        name="ffn",
    )(x2d, mod, w1, b1, w2, b2, g, b)


def _rope_tables(seq):
    rows = seq // GRID_W
    axis_dim = HEAD_DIM // 2
    nf = axis_dim // 2
    inv_freq = ROPE_BASE ** (-jnp.arange(0, axis_dim, 2, dtype=F32) / axis_dim)
    lane = jnp.arange(HEAD_W)
    inv_lane = jnp.tile(inv_freq, HEAD_W // nf)
    ang_r = jnp.arange(rows, dtype=F32)[:, None] * inv_lane
    ang_c = jnp.arange(GRID_W, dtype=F32)[:, None] * inv_lane
    row_lane = (lane % HEAD_DIM) < axis_dim

    def expand(fn):
        return jnp.where(row_lane, fn(ang_r)[:, None, :], fn(ang_c)[None, :, :]).reshape(seq, HEAD_W)

    cos, sin = expand(jnp.cos), expand(jnp.sin)
    first = (lane % (2 * nf)) < nf
    return cos, jnp.where(first, -sin, 0.0), jnp.where(first, 0.0, sin)


def kernel(x, c, ctx, c_ctx, w_ada, b_ada, w_in, b_glu, lambda_q1, lambda_k1, lambda_q2, lambda_k2, subln_g, w_dw, b_dw, conv_ln_g, conv_ln_b, w_out, b_out, ln1_g, ln1_b, w_ff1, b_ff1, w_ff2, b_ff2, ln2_g, ln2_b):
    assert w_ada.shape[0] == DEPTH
    bsz, seq, d = x.shape
    n_ctx = ctx.shape[1]
    attn_w = N_HEADS * HEAD_W
    conv_w = d - attn_w
    assert w_in.shape[2] == 3 * attn_w + 2 * conv_w and seq % GRID_W == 0

    ctx_row = bsz
    c8 = jnp.concatenate([c, c_ctx[None, :], jnp.zeros((8 - bsz - 1, d), F32)], axis=0)
    mod = _ada(c8, w_ada[0], b_ada).reshape(8, 6, d)

    w_in_b = w_in[0].astype(BF16)
    cos, sina, sinb = _rope_tables(seq)
    x2d = x.reshape(bsz * seq, d)
    q, k, vt, u = _inproj(x2d, mod, w_in_b, b_glu, cos, sina, sinb, seq, attn_w, conv_w)
    kc, vct = _inproj_ctx(ctx, mod, w_in_b, ctx_row, attn_w)

    lamv = jnp.concatenate([lambda_q1, lambda_k1, lambda_q2, lambda_k2], axis=0)
    w_dw_p = jnp.concatenate([w_dw[0], jnp.zeros((1, conv_w), F32)], axis=0)
    attn, conv, w_out_b, w_ff1_b, w_ff2_b = _attention(
        lamv, subln_g.reshape(HEAD_W, 1), q.reshape(bsz, seq, attn_w), k.reshape(bsz, seq, attn_w),
        kc, vt, vct, u.reshape(bsz, seq, conv_w), w_dw_p, b_dw, (w_out[0], w_ff1[0], w_ff2[0]))

    x1 = _outproj(attn.reshape(bsz * seq, attn_w), conv.reshape(bsz * seq, conv_w), x2d, mod,
                  w_out_b, b_out, conv_ln_g, conv_ln_b, ln1_g, ln1_b, seq)
    out = _ffn(x1, mod, w_ff1_b, b_ff1, w_ff2_b, b_ff2, ln2_g, ln2_b, seq)
    return out.reshape(bsz, seq, d)
```

```python
import functools
import math

import jax
import jax.numpy as jnp
from jax import lax
from jax.experimental import pallas as pl
from jax.experimental.pallas import tpu as pltpu

F32 = jnp.float32
BF16 = jnp.bfloat16

GRID_W = 64
N_HEADS = 8
HEAD_DIM = 64
HEAD_W = 2 * HEAD_DIM
CONV_KERNEL = 31
CONV_PAD = CONV_KERNEL // 2
ROPE_BASE = 10000.0
LN_EPS = 1e-5
DEPTH = 1
DEEPNORM_ALPHA = (2.0 * DEPTH) ** 0.25
LAM_INIT = 0.8 - 0.6 * math.exp(-0.3 * 0)
Q_SCALE = HEAD_DIM ** -0.5 * math.log2(math.e)

V7X_VMEM_LIMIT = 60 * 1024 * 1024
HALO = 16


def _params(sem, vmem=V7X_VMEM_LIMIT):
    return pltpu.CompilerParams(dimension_semantics=sem, vmem_limit_bytes=vmem)


def _resident(shape):
    return pl.BlockSpec(shape, lambda *_: (0,) * len(shape), pipeline_mode=pl.Buffered(1))


def _normalize(x):
    mu = jnp.mean(x, axis=-1, keepdims=True)
    xc = x - mu
    var = jnp.mean(xc * xc, axis=-1, keepdims=True)
    return xc * lax.rsqrt(var + LN_EPS)


def _sigmoid(x):
    return 1.0 / (1.0 + jnp.exp(-x))


def _ada_kernel(c_ref, w_ref, b_ref, o_ref):
    c = c_ref[...]
    s = (c * _sigmoid(c)).astype(BF16)
    o_ref[...] = jnp.dot(s, w_ref[...].astype(BF16), preferred_element_type=F32) + b_ref[...]


def _ada(c8, w_ada, b_ada, tn=1024):
    rows, d = c8.shape
    n = w_ada.shape[1]
    return pl.pallas_call(
        _ada_kernel,
        grid=(n // tn,),
        in_specs=[pl.BlockSpec((rows, d), lambda j: (0, 0)),
                  pl.BlockSpec((d, tn), lambda j: (0, j)),
                  pl.BlockSpec((1, tn), lambda j: (0, j))],
        out_specs=pl.BlockSpec((rows, tn), lambda j: (0, j)),
        out_shape=jax.ShapeDtypeStruct((rows, n), F32),
        compiler_params=_params(("arbitrary",)),
        name="ada",
    )(c8, w_ada, b_ada)


def _inproj_kernel(x_ref, mod_ref, w_ref, bglu_ref, cos_ref, sina_ref, sinb_ref,
                   q_ref, k_ref, vt_ref, u_ref, *, attn_w, conv_w, chunk, rows):
    mod = mod_ref[0]
    for r0 in range(0, x_ref.shape[0], rows):
        rs = slice(r0, r0 + rows)
        h = (_normalize(x_ref[rs, :]) * (1.0 + mod[1:2]) + mod[0:1]).astype(BF16)
        cos, sina, sinb = cos_ref[rs, :], sina_ref[rs, :], sinb_ref[rs, :]

        def proj(c0):
            return jnp.dot(h, w_ref[:, c0:c0 + chunk], preferred_element_type=F32)

        def rope(p):
            return p * cos + pltpu.roll(p, HEAD_W - 16, axis=1) * sina + pltpu.roll(p, 16, axis=1) * sinb

        for dst, base, scale in ((q_ref, 0, Q_SCALE), (k_ref, attn_w, None)):
            for j in range(attn_w // chunk):
                p = proj(base + j * chunk)
                for hh in range(chunk // HEAD_W):
                    r = rope(p[:, hh * HEAD_W:(hh + 1) * HEAD_W])
                    if scale is not None:
                        r = r * scale
                    c0 = j * chunk + hh * HEAD_W
                    dst[rs, c0:c0 + HEAD_W] = r.astype(BF16)
        for j in range(attn_w // chunk):
            vt_ref[0, j * chunk:(j + 1) * chunk, rs] = proj(2 * attn_w + j * chunk).T.astype(BF16)
        for j in range(conv_w // chunk):
            a = proj(3 * attn_w + j * chunk) + bglu_ref[:, j * chunk:(j + 1) * chunk]
            g = proj(3 * attn_w + conv_w + j * chunk) + bglu_ref[:, conv_w + j * chunk:conv_w + (j + 1) * chunk]
            u_ref[rs, j * chunk:(j + 1) * chunk] = a * _sigmoid(g)


def _inproj(x2d, mod, w_in, b_glu, cos, sina, sinb, seq, attn_w, conv_w, tm=512, chunk=512, rows=256):
    m, d = x2d.shape
    tiles_per_seq = seq // tm
    kern = functools.partial(_inproj_kernel, attn_w=attn_w, conv_w=conv_w, chunk=chunk, rows=rows)
    tab = pl.BlockSpec((tm, HEAD_W), lambda i: (i % tiles_per_seq, 0))
    row = lambda w: pl.BlockSpec((tm, w), lambda i: (i, 0))
    return pl.pallas_call(
        kern,
        grid=(m // tm,),
        in_specs=[row(d),
                  pl.BlockSpec((1,) + mod.shape[1:], lambda i: (i // tiles_per_seq, 0, 0)),
                  _resident(w_in.shape), _resident(b_glu.shape), tab, tab, tab],
        out_specs=[row(attn_w), row(attn_w),
                   pl.BlockSpec((1, attn_w, tm), lambda i: (i // tiles_per_seq, 0, i % tiles_per_seq)),
                   row(conv_w)],
        out_shape=[jax.ShapeDtypeStruct((m, attn_w), BF16)] * 2
                  + [jax.ShapeDtypeStruct((m // seq, attn_w, seq), BF16), jax.ShapeDtypeStruct((m, conv_w), F32)],
        compiler_params=_params(("arbitrary",)),
        name="inproj",
    )(x2d, mod, w_in, b_glu, cos, sina, sinb)


def _inproj_ctx_kernel(x_ref, mod_ref, wk_ref, wv_ref, k_ref, vt_ref, *, chunk):
    mod = mod_ref[0]
    h = (_normalize(x_ref[0]) * (1.0 + mod[1:2]) + mod[0:1]).astype(BF16)
    for j in range(wk_ref.shape[1] // chunk):
        sl = slice(j * chunk, (j + 1) * chunk)
        k_ref[0, :, sl] = jnp.dot(h, wk_ref[:, sl], preferred_element_type=F32).astype(BF16)
        vt_ref[0, sl, :] = jnp.dot(h, wv_ref[:, sl], preferred_element_type=F32).T.astype(BF16)


def _inproj_ctx(ctx, mod, w_in, ctx_row, attn_w, chunk=512):
    bsz, n_ctx, d = ctx.shape
    kern = functools.partial(_inproj_ctx_kernel, chunk=chunk)
    return pl.pallas_call(
        kern,
        grid=(bsz,),
        in_specs=[pl.BlockSpec((1, n_ctx, d), lambda i: (i, 0, 0)),
                  pl.BlockSpec((1,) + mod.shape[1:], lambda i: (ctx_row, 0, 0)),
                  pl.BlockSpec((d, attn_w), lambda i: (0, 1)),
                  pl.BlockSpec((d, attn_w), lambda i: (0, 2))],
        out_specs=[pl.BlockSpec((1, n_ctx, attn_w), lambda i: (i, 0, 0)),
                   pl.BlockSpec((1, attn_w, n_ctx), lambda i: (i, 0, 0))],
        out_shape=[jax.ShapeDtypeStruct((bsz, n_ctx, attn_w), BF16),
                   jax.ShapeDtypeStruct((bsz, attn_w, n_ctx), BF16)],
        compiler_params=_params(("arbitrary",)),
        name="inproj_ctx",
    )(ctx, mod, w_in, w_in)


def _attn_kernel(lamv_ref, g_ref, q_ref, k_ref, kc_ref, vt_ref, vct_ref, o_ref, s_ref, *, tq, tk):
    seq = k_ref.shape[1]
    n_ctx = kc_ref.shape[1]
    lv = lamv_ref[...]
    lam = (jnp.exp(jnp.sum(lv[0:1] * lv[1:2], axis=-1, keepdims=True))
           - jnp.exp(jnp.sum(lv[2:3] * lv[3:4], axis=-1, keepdims=True)) + LAM_INIT)
    lane = lax.broadcasted_iota(jnp.int32, (tq, HEAD_W), 1)
    tkc = min(tk, n_ctx)
    chunks = ([(k_ref, vt_ref, o, tk, o) for o in range(0, seq, tk)]
              + [(kc_ref, vct_ref, o, tkc, seq + o) for o in range(0, n_ctx, tkc)])
    nt = (((1,), (1,)), ((), ()))

    def fold8(acc, x, op):
        for r in range(0, x.shape[0], 8):
            acc = op(acc, x[r:r + 8])
        return acc

    def q_half(tile_idx, c):
        q = q_ref[0, pl.ds(pl.multiple_of(tile_idx * tq, tq), tq), :]
        return jnp.where((lane >= c * HEAD_DIM) & (lane < (c + 1) * HEAD_DIM), q, jnp.zeros_like(q))

    def stage(c_out, m_out, c_in, q_in):
        m8 = jnp.full((8, tq), -jnp.inf, F32)
        l8 = jnp.zeros((8, tq), F32)
        acc = jnp.zeros((HEAD_W, tq), F32)
        for kr, vr, off, n, pos in chunks:
            st = lax.dot_general(kr[0, off:off + n, :], q_in, nt, preferred_element_type=F32)
            s_ref[c_in, pos:pos + n, :] = st
            m8 = fold8(m8, st, jnp.maximum)
            if c_out is not None:
                p = jnp.exp2(s_ref[c_out, pos:pos + n, :] - m_out)
                l8 = fold8(l8, p, jnp.add)
                acc = acc + jnp.dot(vr[0, :, off:off + n], p.astype(BF16), preferred_element_type=F32)
        out = None if c_out is None else acc / jnp.sum(l8, axis=0, keepdims=True)
        return out, jnp.max(m8, axis=0, keepdims=True)

    n_tiles = seq // tq

    def tile(i, m0):
        o0, m1 = stage(0, m0, 1, q_half(i, 1))
        o1, m0_next = stage(1, m1, 0, q_half(jnp.minimum(i + 1, n_tiles - 1), 0))
        ot = o0 - lam * o1
        ot = ot * lax.rsqrt(jnp.mean(ot * ot, axis=0, keepdims=True) + LN_EPS) * g_ref[...] * (1.0 - LAM_INIT)
        o_ref[0, pl.ds(pl.multiple_of(i * tq, tq), tq), :] = ot.T.astype(BF16)
        return m0_next

    _, m_first = stage(None, None, 0, q_half(0, 0))
    lax.fori_loop(0, n_tiles, tile, m_first)


def _attention(lamv, subln_g_col, q, k, kc, vt, vct, tq=256, tk=256):
    b, seq, attn_w = q.shape
    n_ctx = kc.shape[1]
    kern = functools.partial(_attn_kernel, tq=tq, tk=tk)
    head = lambda n: pl.BlockSpec((1, n, HEAD_W), lambda bi, hi: (bi, 0, hi))
    head_t = lambda n: pl.BlockSpec((1, HEAD_W, n), lambda bi, hi: (bi, hi, 0))
    return pl.pallas_call(
        kern,
        grid=(b, attn_w // HEAD_W),
        in_specs=[pl.BlockSpec(lamv.shape, lambda bi, hi: (0, 0)),
                  pl.BlockSpec(subln_g_col.shape, lambda bi, hi: (0, 0)),
                  head(seq), head(seq), head(n_ctx), head_t(seq), head_t(n_ctx)],
        out_specs=head(seq),
        out_shape=jax.ShapeDtypeStruct((b, seq, attn_w), BF16),
        scratch_shapes=[pltpu.VMEM((2, seq + n_ctx, tq), F32)],
        compiler_params=_params(("arbitrary", "arbitrary")),
        name="attn",
    )(lamv, subln_g_col, q, k, kc, vt, vct)


def _conv_kernel(u_ref, prev_ref, next_ref, w_ref, bdw_ref, g_ref, b_ref, wo_ref, w1_ref, w2_ref,
                 o_ref, wo16_ref, w116_ref, w216_ref, buf_ref, acc_ref, *, tl, rows, lanes):
    for src, dst in ((wo_ref, wo16_ref), (w1_ref, w116_ref), (w2_ref, w216_ref)):
        dst[...] = src[...].astype(BF16)

    i = pl.program_id(1)
    last = pl.num_programs(1) - 1
    width = u_ref.shape[2]
    buf_ref[HALO:HALO + tl, :] = u_ref[0]
    buf_ref[0:HALO, :] = jnp.where(i > 0, prev_ref[0], 0.0)
    buf_ref[HALO + tl:, :] = jnp.where(i < last, next_ref[0], 0.0)
    span = rows + 8 * ((CONV_KERNEL - 1) // 8)
    win = span + 8

    def row_block(rb, carry):
        r0 = pl.multiple_of(rb * rows, rows)
        for c0 in range(0, width, lanes):
            acc = jnp.zeros((rows, lanes), F32)
            xa = buf_ref[pl.ds(r0, win), c0:c0 + lanes]
            for sub in range(8):
                shift = HALO - CONV_PAD + sub
                xs = pltpu.roll(xa, win - shift, axis=0) if shift % 8 else xa[shift:]
                for t in range(sub, CONV_KERNEL, 8):
                    acc = acc + w_ref[t:t + 1, c0:c0 + lanes] * xs[t - sub:t - sub + rows]
            acc_ref[pl.ds(r0, rows), c0:c0 + lanes] = acc + bdw_ref[:, c0:c0 + lanes]
        return carry

    lax.fori_loop(0, tl // rows, row_block, 0)
    y = _normalize(acc_ref[...]) * g_ref[...] + b_ref[...]
    o_ref[0] = (y * _sigmoid(y)).astype(BF16)


def _conv(u, w_dw, b_dw, g, b, cast_weights, tl=512, rows=64, lanes=128):
    bsz, seq, width = u.shape
    halo_per_tile = tl // HALO
    n_halo = seq // HALO
    tiles = seq // tl
    steps = bsz * tiles
    assert len(cast_weights) == 3
    kern = functools.partial(_conv_kernel, tl=tl, rows=rows, lanes=lanes)
    vec = pl.BlockSpec((1, width), lambda bi, i: (0, 0))
    slab = lambda w: pl.BlockSpec((w.shape[0] // steps, w.shape[1]), lambda bi, i: (bi * tiles + i, 0))
    return pl.pallas_call(
        kern,
        grid=(bsz, tiles),
        in_specs=[pl.BlockSpec((1, tl, width), lambda bi, i: (bi, i, 0)),
                  pl.BlockSpec((1, HALO, width), lambda bi, i: (bi, jnp.maximum(i * halo_per_tile - 1, 0), 0)),
                  pl.BlockSpec((1, HALO, width),
                               lambda bi, i: (bi, jnp.minimum((i + 1) * halo_per_tile, n_halo - 1), 0)),
                  pl.BlockSpec(w_dw.shape, lambda bi, i: (0, 0)), vec, vec, vec]
                 + [slab(w) for w in cast_weights],
        out_specs=[pl.BlockSpec((1, tl, width), lambda bi, i: (bi, i, 0))] + [slab(w) for w in cast_weights],
        out_shape=[jax.ShapeDtypeStruct((bsz, seq, width), BF16)]
                  + [jax.ShapeDtypeStruct(w.shape, BF16) for w in cast_weights],
        scratch_shapes=[pltpu.VMEM((tl + 2 * HALO, width), F32), pltpu.VMEM((tl, width), F32)],
        compiler_params=_params(("arbitrary", "arbitrary")),
        name="conv",
    )(u, u, u, w_dw, b_dw, g, b, *cast_weights)


def _outproj_kernel(a_ref, c_ref, x_ref, mod_ref, w_ref, bo_ref, g_ref, b_ref, o_ref, z_ref, *, chunk, rows):
    attn_w = a_ref.shape[1]
    gate = mod_ref[0, 2:3, :]
    for r0 in range(0, x_ref.shape[0], rows):
        rs = slice(r0, r0 + rows)
        a = a_ref[rs, :]
        cv = c_ref[rs, :]
        for c0 in range(0, w_ref.shape[1], chunk):
            sl = slice(c0, c0 + chunk)
            y = (jnp.dot(a, w_ref[0:attn_w, sl], preferred_element_type=F32)
                 + jnp.dot(cv, w_ref[attn_w:, sl], preferred_element_type=F32) + bo_ref[:, sl])
            z_ref[rs, sl] = DEEPNORM_ALPHA * x_ref[rs, sl] + gate[:, sl] * y
        o_ref[rs, :] = _normalize(z_ref[rs, :]) * g_ref[...] + b_ref[...]


def _outproj(attn2d, conv2d, x2d, mod, w_out, b_out, g, b, seq, tm=512, chunk=512, rows=256):
    m, d = x2d.shape
    tiles_per_seq = seq // tm
    kern = functools.partial(_outproj_kernel, chunk=chunk, rows=rows)
    row = lambda w: pl.BlockSpec((tm, w), lambda i: (i, 0))
    vec = pl.BlockSpec((1, d), lambda i: (0, 0))
    return pl.pallas_call(
        kern,
        grid=(m // tm,),
        in_specs=[row(attn2d.shape[1]), row(conv2d.shape[1]), row(d),
                  pl.BlockSpec((1,) + mod.shape[1:], lambda i: (i // tiles_per_seq, 0, 0)),
                  _resident(w_out.shape), vec, vec, vec],
        out_specs=row(d),
        out_shape=jax.ShapeDtypeStruct((m, d), F32),
        scratch_shapes=[pltpu.VMEM((tm, d), F32)],
        compiler_params=_params(("arbitrary",)),
        name="outproj",
    )(attn2d, conv2d, x2d, mod, w_out, b_out, g, b)


def _ffn_kernel(x_ref, mod_ref, w1_ref, b1_ref, w2_ref, b2_ref, g_ref, b_ref, o_ref, h_ref, *, rows):
    f = pl.program_id(1)
    last = pl.num_programs(1) - 1
    mod = mod_ref[0]

    def step(first, final):
        for r0 in range(0, x_ref.shape[0], rows):
            rs = slice(r0, r0 + rows)
            if first:
                h = (_normalize(x_ref[rs, :]) * (1.0 + mod[4:5]) + mod[3:4]).astype(BF16)
                h_ref[rs, :] = h
            else:
                h = h_ref[rs, :]
            t = jnp.maximum(jnp.dot(h, w1_ref[...], preferred_element_type=F32) + b1_ref[...], 0.0)
            y = jnp.dot((t * t).astype(BF16), w2_ref[...], preferred_element_type=F32)
            if not first:
                y = o_ref[rs, :] + y
            if final:
                z = DEEPNORM_ALPHA * x_ref[rs, :] + mod[5:6] * (y + b2_ref[...])
                y = _normalize(z) * g_ref[...] + b_ref[...]
            o_ref[rs, :] = y

    pl.when(f == 0)(functools.partial(step, True, False))
    pl.when((f > 0) & (f < last))(functools.partial(step, False, False))
    pl.when(f == last)(functools.partial(step, False, True))


def _ffn(x2d, mod, w1, b1, w2, b2, g, b, seq, tm=1024, tf=1024, rows=512):
    m, d = x2d.shape
    dff = w1.shape[1]
    tiles_per_seq = seq // tm
    assert dff // tf >= 2
    vec = pl.BlockSpec((1, d), lambda i, f: (0, 0))
    return pl.pallas_call(
        functools.partial(_ffn_kernel, rows=rows),
        grid=(m // tm, dff // tf),
        in_specs=[pl.BlockSpec((tm, d), lambda i, f: (i, 0)),
                  pl.BlockSpec((1,) + mod.shape[1:], lambda i, f: (i // tiles_per_seq, 0, 0)),
                  pl.BlockSpec((d, tf), lambda i, f: (0, f)),
                  pl.BlockSpec((1, tf), lambda i, f: (0, f)),
                  pl.BlockSpec((tf, d), lambda i, f: (f, 0)),
                  vec, vec, vec],
        out_specs=pl.BlockSpec((tm, d), lambda i, f: (i, 0)),
        out_shape=jax.ShapeDtypeStruct((m, d), F32),
        scratch_shapes=[pltpu.VMEM((tm, d), BF16)],
        compiler_params=_params(("arbitrary", "arbitrary")),
        name="ffn",
    )(x2d, mod, w1, b1, w2, b2, g, b)


def _rope_tables(seq):
    rows = seq // GRID_W
    axis_dim = HEAD_DIM // 2
    nf = axis_dim // 2
    inv_freq = ROPE_BASE ** (-jnp.arange(0, axis_dim, 2, dtype=F32) / axis_dim)
    lane = jnp.arange(HEAD_W)
    inv_lane = jnp.tile(inv_freq, HEAD_W // nf)
    ang_r = jnp.arange(rows, dtype=F32)[:, None] * inv_lane
    ang_c = jnp.arange(GRID_W, dtype=F32)[:, None] * inv_lane
    row_lane = (lane % HEAD_DIM) < axis_dim

    def expand(fn):
        return jnp.where(row_lane, fn(ang_r)[:, None, :], fn(ang_c)[None, :, :]).reshape(seq, HEAD_W)

    cos, sin = expand(jnp.cos), expand(jnp.sin)
    first = (lane % (2 * nf)) < nf
    return cos, jnp.where(first, -sin, 0.0), jnp.where(first, 0.0, sin)


def kernel(x, c, ctx, c_ctx, w_ada, b_ada, w_in, b_glu, lambda_q1, lambda_k1, lambda_q2, lambda_k2, subln_g, w_dw, b_dw, conv_ln_g, conv_ln_b, w_out, b_out, ln1_g, ln1_b, w_ff1, b_ff1, w_ff2, b_ff2, ln2_g, ln2_b):
    assert w_ada.shape[0] == DEPTH
    bsz, seq, d = x.shape
    n_ctx = ctx.shape[1]
    attn_w = N_HEADS * HEAD_W
    conv_w = d - attn_w
    assert w_in.shape[2] == 3 * attn_w + 2 * conv_w and seq % GRID_W == 0

    ctx_row = bsz
    c8 = jnp.concatenate([c, c_ctx[None, :], jnp.zeros((8 - bsz - 1, d), F32)], axis=0)
    mod = _ada(c8, w_ada[0], b_ada).reshape(8, 6, d)

    w_in_b = w_in[0].astype(BF16)
    cos, sina, sinb = _rope_tables(seq)
    x2d = x.reshape(bsz * seq, d)
    q, k, vt, u = _inproj(x2d, mod, w_in_b, b_glu, cos, sina, sinb, seq, attn_w, conv_w)
    kc, vct = _inproj_ctx(ctx, mod, w_in_b, ctx_row, attn_w)

    lamv = jnp.concatenate([lambda_q1, lambda_k1, lambda_q2, lambda_k2], axis=0)
    attn = _attention(lamv, subln_g.reshape(HEAD_W, 1), q.reshape(bsz, seq, attn_w),
                      k.reshape(bsz, seq, attn_w), kc, vt, vct)

    w_dw_p = jnp.concatenate([w_dw[0], jnp.zeros((1, conv_w), F32)], axis=0)
    conv, w_out_b, w_ff1_b, w_ff2_b = _conv(u.reshape(bsz, seq, conv_w), w_dw_p, b_dw, conv_ln_g, conv_ln_b,
                                            (w_out[0], w_ff1[0], w_ff2[0]))

    x1 = _outproj(attn.reshape(bsz * seq, attn_w), conv.reshape(bsz * seq, conv_w), x2d, mod,
                  w_out_b, b_out, ln1_g, ln1_b, seq)
    out = _ffn(x1, mod, w_ff1_b, b_ff1, w_ff2_b, b_ff2, ln2_g, ln2_b, seq)
    return out.reshape(bsz, seq, d)
```

```python
import functools
import math

import jax
import jax.numpy as jnp
from jax import lax
from jax.experimental import pallas as pl
from jax.experimental.pallas import tpu as pltpu

F32 = jnp.float32
BF16 = jnp.bfloat16

GRID_W = 64
N_HEADS = 8
HEAD_DIM = 64
HEAD_W = 2 * HEAD_DIM
CONV_KERNEL = 31
CONV_PAD = CONV_KERNEL // 2
ROPE_BASE = 10000.0
LN_EPS = 1e-5
DEPTH = 1
DEEPNORM_ALPHA = (2.0 * DEPTH) ** 0.25
LAM_INIT = 0.8 - 0.6 * math.exp(-0.3 * 0)
Q_SCALE = HEAD_DIM ** -0.5 * math.log2(math.e)

V7X_VMEM_LIMIT = 60 * 1024 * 1024
HALO = 16
PV_LEAD = 2


def _params(sem, vmem=V7X_VMEM_LIMIT):
    return pltpu.CompilerParams(dimension_semantics=sem, vmem_limit_bytes=vmem)


def _resident(shape):
    return pl.BlockSpec(shape, lambda *_: (0,) * len(shape), pipeline_mode=pl.Buffered(1))


def _normalize(x):
    mu = jnp.mean(x, axis=-1, keepdims=True)
    xc = x - mu
    var = jnp.mean(xc * xc, axis=-1, keepdims=True)
    return xc * lax.rsqrt(var + LN_EPS)


def _sigmoid(x):
    return 1.0 / (1.0 + jnp.exp(-x))


def _ada_kernel(c_ref, w_ref, b_ref, o_ref):
    c = c_ref[...]
    s = (c * _sigmoid(c)).astype(BF16)
    o_ref[...] = jnp.dot(s, w_ref[...].astype(BF16), preferred_element_type=F32) + b_ref[...]


def _ada(c8, w_ada, b_ada, tn=1024):
    rows, d = c8.shape
    n = w_ada.shape[1]
    return pl.pallas_call(
        _ada_kernel,
        grid=(n // tn,),
        in_specs=[pl.BlockSpec((rows, d), lambda j: (0, 0)),
                  pl.BlockSpec((d, tn), lambda j: (0, j)),
                  pl.BlockSpec((1, tn), lambda j: (0, j))],
        out_specs=pl.BlockSpec((rows, tn), lambda j: (0, j)),
        out_shape=jax.ShapeDtypeStruct((rows, n), F32),
        compiler_params=_params(("arbitrary",)),
        name="ada",
    )(c8, w_ada, b_ada)


def _inproj_kernel(x_ref, mod_ref, w_ref, bglu_ref, cos_ref, sina_ref, sinb_ref,
                   q_ref, k_ref, vt_ref, u_ref, *, attn_w, conv_w, chunk, rows):
    mod = mod_ref[0]
    for r0 in range(0, x_ref.shape[0], rows):
        rs = slice(r0, r0 + rows)
        h = (_normalize(x_ref[rs, :]) * (1.0 + mod[1:2]) + mod[0:1]).astype(BF16)
        cos, sina, sinb = cos_ref[rs, :], sina_ref[rs, :], sinb_ref[rs, :]

        def proj(c0):
            return jnp.dot(h, w_ref[:, c0:c0 + chunk], preferred_element_type=F32)

        def rope(p):
            return p * cos + pltpu.roll(p, HEAD_W - 16, axis=1) * sina + pltpu.roll(p, 16, axis=1) * sinb

        for dst, base, scale in ((q_ref, 0, Q_SCALE), (k_ref, attn_w, None)):
            for j in range(attn_w // chunk):
                p = proj(base + j * chunk)
                for hh in range(chunk // HEAD_W):
                    r = rope(p[:, hh * HEAD_W:(hh + 1) * HEAD_W])
                    if scale is not None:
                        r = r * scale
                    c0 = j * chunk + hh * HEAD_W
                    dst[rs, c0:c0 + HEAD_W] = r.astype(BF16)
        for j in range(attn_w // chunk):
            vt_ref[0, j * chunk:(j + 1) * chunk, rs] = proj(2 * attn_w + j * chunk).T.astype(BF16)
        for j in range(conv_w // chunk):
            a = proj(3 * attn_w + j * chunk) + bglu_ref[:, j * chunk:(j + 1) * chunk]
            g = proj(3 * attn_w + conv_w + j * chunk) + bglu_ref[:, conv_w + j * chunk:conv_w + (j + 1) * chunk]
            u_ref[rs, j * chunk:(j + 1) * chunk] = a * _sigmoid(g)


def _inproj(x2d, mod, w_in, b_glu, cos, sina, sinb, seq, attn_w, conv_w, tm=512, chunk=512, rows=256):
    m, d = x2d.shape
    tiles_per_seq = seq // tm
    kern = functools.partial(_inproj_kernel, attn_w=attn_w, conv_w=conv_w, chunk=chunk, rows=rows)
    tab = pl.BlockSpec((tm, HEAD_W), lambda i: (i % tiles_per_seq, 0))
    row = lambda w: pl.BlockSpec((tm, w), lambda i: (i, 0))
    return pl.pallas_call(
        kern,
        grid=(m // tm,),
        in_specs=[row(d),
                  pl.BlockSpec((1,) + mod.shape[1:], lambda i: (i // tiles_per_seq, 0, 0)),
                  _resident(w_in.shape), _resident(b_glu.shape), tab, tab, tab],
        out_specs=[row(attn_w), row(attn_w),
                   pl.BlockSpec((1, attn_w, tm), lambda i: (i // tiles_per_seq, 0, i % tiles_per_seq)),
                   row(conv_w)],
        out_shape=[jax.ShapeDtypeStruct((m, attn_w), BF16)] * 2
                  + [jax.ShapeDtypeStruct((m // seq, attn_w, seq), BF16), jax.ShapeDtypeStruct((m, conv_w), F32)],
        compiler_params=_params(("arbitrary",)),
        name="inproj",
    )(x2d, mod, w_in, b_glu, cos, sina, sinb)


def _inproj_ctx_kernel(x_ref, mod_ref, wk_ref, wv_ref, k_ref, vt_ref, *, chunk):
    mod = mod_ref[0]
    h = (_normalize(x_ref[0]) * (1.0 + mod[1:2]) + mod[0:1]).astype(BF16)
    for j in range(wk_ref.shape[1] // chunk):
        sl = slice(j * chunk, (j + 1) * chunk)
        k_ref[0, :, sl] = jnp.dot(h, wk_ref[:, sl], preferred_element_type=F32).astype(BF16)
        vt_ref[0, sl, :] = jnp.dot(h, wv_ref[:, sl], preferred_element_type=F32).T.astype(BF16)


def _inproj_ctx(ctx, mod, w_in, ctx_row, attn_w, chunk=512):
    bsz, n_ctx, d = ctx.shape
    kern = functools.partial(_inproj_ctx_kernel, chunk=chunk)
    return pl.pallas_call(
        kern,
        grid=(bsz,),
        in_specs=[pl.BlockSpec((1, n_ctx, d), lambda i: (i, 0, 0)),
                  pl.BlockSpec((1,) + mod.shape[1:], lambda i: (ctx_row, 0, 0)),
                  pl.BlockSpec((d, attn_w), lambda i: (0, 1)),
                  pl.BlockSpec((d, attn_w), lambda i: (0, 2))],
        out_specs=[pl.BlockSpec((1, n_ctx, attn_w), lambda i: (i, 0, 0)),
                   pl.BlockSpec((1, attn_w, n_ctx), lambda i: (i, 0, 0))],
        out_shape=[jax.ShapeDtypeStruct((bsz, n_ctx, attn_w), BF16),
                   jax.ShapeDtypeStruct((bsz, attn_w, n_ctx), BF16)],
        compiler_params=_params(("arbitrary",)),
        name="inproj_ctx",
    )(ctx, mod, w_in, w_in)


def _attn_kernel(lamv_ref, g_ref, q_ref, k_ref, kc_ref, vt_ref, vct_ref, o_ref, s_ref, *, tq, tk):
    seq = k_ref.shape[1]
    n_ctx = kc_ref.shape[1]
    lv = lamv_ref[...]
    lam = (jnp.exp(jnp.sum(lv[0:1] * lv[1:2], axis=-1, keepdims=True))
           - jnp.exp(jnp.sum(lv[2:3] * lv[3:4], axis=-1, keepdims=True)) + LAM_INIT)
    lane = lax.broadcasted_iota(jnp.int32, (tq, HEAD_W), 1)
    tkc = min(tk, n_ctx)
    chunks = ([(k_ref, vt_ref, o, tk, o) for o in range(0, seq, tk)]
              + [(kc_ref, vct_ref, o, tkc, seq + o) for o in range(0, n_ctx, tkc)])
    nt = (((1,), (1,)), ((), ()))

    def fold8(acc, x, op):
        for r in range(0, x.shape[0], 8):
            acc = op(acc, x[r:r + 8])
        return acc

    def q_half(tile_idx, c):
        q = q_ref[0, pl.ds(pl.multiple_of(tile_idx * tq, tq), tq), :]
        return jnp.where((lane >= c * HEAD_DIM) & (lane < (c + 1) * HEAD_DIM), q, jnp.zeros_like(q))

    def stage(c_out, m_out, c_in, q_in):
        m8 = jnp.full((8, tq), -jnp.inf, F32)
        l8 = jnp.zeros((8, tq), F32)
        acc = jnp.zeros((HEAD_W, tq), F32)
        for j in range(len(chunks) + PV_LEAD):
            if c_out is not None and j < len(chunks):
                _, vr, off, n, pos = chunks[j]
                p = jnp.exp2(s_ref[c_out, pos:pos + n, :] - m_out)
                l8 = fold8(l8, p, jnp.add)
                acc = acc + jnp.dot(vr[0, :, off:off + n], p.astype(BF16), preferred_element_type=F32)
            if j >= PV_LEAD:
                kr, _, off, n, pos = chunks[j - PV_LEAD]
                st = lax.dot_general(kr[0, off:off + n, :], q_in, nt, preferred_element_type=F32)
                s_ref[c_in, pos:pos + n, :] = st
                m8 = fold8(m8, st, jnp.maximum)
        out = None if c_out is None else acc / jnp.sum(l8, axis=0, keepdims=True)
        return out, jnp.max(m8, axis=0, keepdims=True)

    n_tiles = seq // tq

    def tile(i, m0):
        o0, m1 = stage(0, m0, 1, q_half(i, 1))
        o1, m0_next = stage(1, m1, 0, q_half(jnp.minimum(i + 1, n_tiles - 1), 0))
        ot = o0 - lam * o1
        ot = ot * lax.rsqrt(jnp.mean(ot * ot, axis=0, keepdims=True) + LN_EPS) * g_ref[...] * (1.0 - LAM_INIT)
        o_ref[0, pl.ds(pl.multiple_of(i * tq, tq), tq), :] = ot.T.astype(BF16)
        return m0_next

    _, m_first = stage(None, None, 0, q_half(0, 0))
    lax.fori_loop(0, n_tiles, tile, m_first)


def _attention(lamv, subln_g_col, q, k, kc, vt, vct, tq=256, tk=256):
    b, seq, attn_w = q.shape
    n_ctx = kc.shape[1]
    kern = functools.partial(_attn_kernel, tq=tq, tk=tk)
    head = lambda n: pl.BlockSpec((1, n, HEAD_W), lambda bi, hi: (bi, 0, hi))
    head_t = lambda n: pl.BlockSpec((1, HEAD_W, n), lambda bi, hi: (bi, hi, 0))
    return pl.pallas_call(
        kern,
        grid=(b, attn_w // HEAD_W),
        in_specs=[pl.BlockSpec(lamv.shape, lambda bi, hi: (0, 0)),
                  pl.BlockSpec(subln_g_col.shape, lambda bi, hi: (0, 0)),
                  head(seq), head(seq), head(n_ctx), head_t(seq), head_t(n_ctx)],
        out_specs=head(seq),
        out_shape=jax.ShapeDtypeStruct((b, seq, attn_w), BF16),
        scratch_shapes=[pltpu.VMEM((2, seq + n_ctx, tq), F32)],
        compiler_params=_params(("arbitrary", "arbitrary")),
        name="attn",
    )(lamv, subln_g_col, q, k, kc, vt, vct)


def _conv_kernel(u_ref, prev_ref, next_ref, w_ref, bdw_ref, g_ref, b_ref, wo_ref, w1_ref, w2_ref,
                 o_ref, wo16_ref, w116_ref, w216_ref, buf_ref, acc_ref, *, tl, rows, lanes):
    for src, dst in ((wo_ref, wo16_ref), (w1_ref, w116_ref), (w2_ref, w216_ref)):
        dst[...] = src[...].astype(BF16)

    i = pl.program_id(1)
    last = pl.num_programs(1) - 1
    width = u_ref.shape[2]
    buf_ref[HALO:HALO + tl, :] = u_ref[0]
    buf_ref[0:HALO, :] = jnp.where(i > 0, prev_ref[0], 0.0)
    buf_ref[HALO + tl:, :] = jnp.where(i < last, next_ref[0], 0.0)
    span = rows + 8 * ((CONV_KERNEL - 1) // 8)
    win = span + 8

    def row_block(rb, carry):
        r0 = pl.multiple_of(rb * rows, rows)
        for c0 in range(0, width, lanes):
            acc = jnp.zeros((rows, lanes), F32)
            xa = buf_ref[pl.ds(r0, win), c0:c0 + lanes]
            for sub in range(8):
                shift = HALO - CONV_PAD + sub
                xs = pltpu.roll(xa, win - shift, axis=0) if shift % 8 else xa[shift:]
                for t in range(sub, CONV_KERNEL, 8):
                    acc = acc + w_ref[t:t + 1, c0:c0 + lanes] * xs[t - sub:t - sub + rows]
            acc_ref[pl.ds(r0, rows), c0:c0 + lanes] = acc + bdw_ref[:, c0:c0 + lanes]
        return carry

    lax.fori_loop(0, tl // rows, row_block, 0)
    y = _normalize(acc_ref[...]) * g_ref[...] + b_ref[...]
    o_ref[0] = (y * _sigmoid(y)).astype(BF16)


def _conv(u, w_dw, b_dw, g, b, cast_weights, tl=512, rows=64, lanes=128):
    bsz, seq, width = u.shape
    halo_per_tile = tl // HALO
    n_halo = seq // HALO
    tiles = seq // tl
    steps = bsz * tiles
    assert len(cast_weights) == 3
    kern = functools.partial(_conv_kernel, tl=tl, rows=rows, lanes=lanes)
    vec = pl.BlockSpec((1, width), lambda bi, i: (0, 0))
    slab = lambda w: pl.BlockSpec((w.shape[0] // steps, w.shape[1]), lambda bi, i: (bi * tiles + i, 0))
    return pl.pallas_call(
        kern,
        grid=(bsz, tiles),
        in_specs=[pl.BlockSpec((1, tl, width), lambda bi, i: (bi, i, 0)),
                  pl.BlockSpec((1, HALO, width), lambda bi, i: (bi, jnp.maximum(i * halo_per_tile - 1, 0), 0)),
                  pl.BlockSpec((1, HALO, width),
                               lambda bi, i: (bi, jnp.minimum((i + 1) * halo_per_tile, n_halo - 1), 0)),
                  pl.BlockSpec(w_dw.shape, lambda bi, i: (0, 0)), vec, vec, vec]
                 + [slab(w) for w in cast_weights],
        out_specs=[pl.BlockSpec((1, tl, width), lambda bi, i: (bi, i, 0))] + [slab(w) for w in cast_weights],
        out_shape=[jax.ShapeDtypeStruct((bsz, seq, width), BF16)]
                  + [jax.ShapeDtypeStruct(w.shape, BF16) for w in cast_weights],
        scratch_shapes=[pltpu.VMEM((tl + 2 * HALO, width), F32), pltpu.VMEM((tl, width), F32)],
        compiler_params=_params(("arbitrary", "arbitrary")),
        name="conv",
    )(u, u, u, w_dw, b_dw, g, b, *cast_weights)


def _outproj_kernel(a_ref, c_ref, x_ref, mod_ref, w_ref, bo_ref, g_ref, b_ref, o_ref, z_ref, *, chunk, rows):
    attn_w = a_ref.shape[1]
    gate = mod_ref[0, 2:3, :]
    for r0 in range(0, x_ref.shape[0], rows):
        rs = slice(r0, r0 + rows)
        a = a_ref[rs, :]
        cv = c_ref[rs, :]
        for c0 in range(0, w_ref.shape[1], chunk):
            sl = slice(c0, c0 + chunk)
            y = (jnp.dot(a, w_ref[0:attn_w, sl], preferred_element_type=F32)
                 + jnp.dot(cv, w_ref[attn_w:, sl], preferred_element_type=F32) + bo_ref[:, sl])
            z_ref[rs, sl] = DEEPNORM_ALPHA * x_ref[rs, sl] + gate[:, sl] * y
        o_ref[rs, :] = _normalize(z_ref[rs, :]) * g_ref[...] + b_ref[...]


def _outproj(attn2d, conv2d, x2d, mod, w_out, b_out, g, b, seq, tm=512, chunk=512, rows=256):
    m, d = x2d.shape
    tiles_per_seq = seq // tm
    kern = functools.partial(_outproj_kernel, chunk=chunk, rows=rows)
    row = lambda w: pl.BlockSpec((tm, w), lambda i: (i, 0))
    vec = pl.BlockSpec((1, d), lambda i: (0, 0))
    return pl.pallas_call(
        kern,
        grid=(m // tm,),
        in_specs=[row(attn2d.shape[1]), row(conv2d.shape[1]), row(d),
                  pl.BlockSpec((1,) + mod.shape[1:], lambda i: (i // tiles_per_seq, 0, 0)),
                  _resident(w_out.shape), vec, vec, vec],
        out_specs=row(d),
        out_shape=jax.ShapeDtypeStruct((m, d), F32),
        scratch_shapes=[pltpu.VMEM((tm, d), F32)],
        compiler_params=_params(("arbitrary",)),
        name="outproj",
    )(attn2d, conv2d, x2d, mod, w_out, b_out, g, b)


def _ffn_kernel(x_ref, mod_ref, w1_ref, b1_ref, w2_ref, b2_ref, g_ref, b_ref, o_ref, h_ref, *, rows):
    f = pl.program_id(1)
    last = pl.num_programs(1) - 1
    mod = mod_ref[0]

    def step(first, final):
        for r0 in range(0, x_ref.shape[0], rows):
            rs = slice(r0, r0 + rows)
            if first:
                h = (_normalize(x_ref[rs, :]) * (1.0 + mod[4:5]) + mod[3:4]).astype(BF16)
                h_ref[rs, :] = h
            else:
                h = h_ref[rs, :]
            t = jnp.maximum(jnp.dot(h, w1_ref[...], preferred_element_type=F32) + b1_ref[...], 0.0)
            y = jnp.dot((t * t).astype(BF16), w2_ref[...], preferred_element_type=F32)
            if not first:
                y = o_ref[rs, :] + y
            if final:
                z = DEEPNORM_ALPHA * x_ref[rs, :] + mod[5:6] * (y + b2_ref[...])
                y = _normalize(z) * g_ref[...] + b_ref[...]
            o_ref[rs, :] = y

    pl.when(f == 0)(functools.partial(step, True, False))
    pl.when((f > 0) & (f < last))(functools.partial(step, False, False))
    pl.when(f == last)(functools.partial(step, False, True))


def _ffn(x2d, mod, w1, b1, w2, b2, g, b, seq, tm=1024, tf=1024, rows=512):
    m, d = x2d.shape
    dff = w1.shape[1]
    tiles_per_seq = seq // tm
    assert dff // tf >= 2
    vec = pl.BlockSpec((1, d), lambda i, f: (0, 0))
    return pl.pallas_call(
        functools.partial(_ffn_kernel, rows=rows),
        grid=(m // tm, dff // tf),
        in_specs=[pl.BlockSpec((tm, d), lambda i, f: (i, 0)),
                  pl.BlockSpec((1,) + mod.shape[1:], lambda i, f: (i // tiles_per_seq, 0, 0)),
                  pl.BlockSpec((d, tf), lambda i, f: (0, f)),
                  pl.BlockSpec((1, tf), lambda i, f: (0, f)),
                  pl.BlockSpec((tf, d), lambda i, f: (f, 0)),
                  vec, vec, vec],
        out_specs=pl.BlockSpec((tm, d), lambda i, f: (i, 0)),
        out_shape=jax.ShapeDtypeStruct((m, d), F32),
        scratch_shapes=[pltpu.VMEM((tm, d), BF16)],
        compiler_params=_params(("arbitrary", "arbitrary")),
        name="ffn",
    )(x2d, mod, w1, b1, w2, b2, g, b)


def _rope_tables(seq):
    rows = seq // GRID_W
    axis_dim = HEAD_DIM // 2
    nf = axis_dim // 2
    inv_freq = ROPE_BASE ** (-jnp.arange(0, axis_dim, 2, dtype=F32) / axis_dim)
    lane = jnp.arange(HEAD_W)
    inv_lane = jnp.tile(inv_freq, HEAD_W // nf)
    ang_r = jnp.arange(rows, dtype=F32)[:, None] * inv_lane
    ang_c = jnp.arange(GRID_W, dtype=F32)[:, None] * inv_lane
    row_lane = (lane % HEAD_DIM) < axis_dim

    def expand(fn):
        return jnp.where(row_lane, fn(ang_r)[:, None, :], fn(ang_c)[None, :, :]).reshape(seq, HEAD_W)

    cos, sin = expand(jnp.cos), expand(jnp.sin)
    first = (lane % (2 * nf)) < nf
    return cos, jnp.where(first, -sin, 0.0), jnp.where(first, 0.0, sin)


def kernel(x, c, ctx, c_ctx, w_ada, b_ada, w_in, b_glu, lambda_q1, lambda_k1, lambda_q2, lambda_k2, subln_g, w_dw, b_dw, conv_ln_g, conv_ln_b, w_out, b_out, ln1_g, ln1_b, w_ff1, b_ff1, w_ff2, b_ff2, ln2_g, ln2_b):
    assert w_ada.shape[0] == DEPTH
    bsz, seq, d = x.shape
    n_ctx = ctx.shape[1]
    attn_w = N_HEADS * HEAD_W
    conv_w = d - attn_w
    assert w_in.shape[2] == 3 * attn_w + 2 * conv_w and seq % GRID_W == 0

    ctx_row = bsz
    c8 = jnp.concatenate([c, c_ctx[None, :], jnp.zeros((8 - bsz - 1, d), F32)], axis=0)
    mod = _ada(c8, w_ada[0], b_ada).reshape(8, 6, d)

    w_in_b = w_in[0].astype(BF16)
    cos, sina, sinb = _rope_tables(seq)
    x2d = x.reshape(bsz * seq, d)
    q, k, vt, u = _inproj(x2d, mod, w_in_b, b_glu, cos, sina, sinb, seq, attn_w, conv_w)
    kc, vct = _inproj_ctx(ctx, mod, w_in_b, ctx_row, attn_w)

    lamv = jnp.concatenate([lambda_q1, lambda_k1, lambda_q2, lambda_k2], axis=0)
    attn = _attention(lamv, subln_g.reshape(HEAD_W, 1), q.reshape(bsz, seq, attn_w),
                      k.reshape(bsz, seq, attn_w), kc, vt, vct)

    w_dw_p = jnp.concatenate([w_dw[0], jnp.zeros((1, conv_w), F32)], axis=0)
    conv, w_out_b, w_ff1_b, w_ff2_b = _conv(u.reshape(bsz, seq, conv_w), w_dw_p, b_dw, conv_ln_g, conv_ln_b,
                                            (w_out[0], w_ff1[0], w_ff2[0]))

    x1 = _outproj(attn.reshape(bsz * seq, attn_w), conv.reshape(bsz * seq, conv_w), x2d, mod,
                  w_out_b, b_out, ln1_g, ln1_b, seq)
    out = _ffn(x1, mod, w_ff1_b, b_ff1, w_ff2_b, b_ff2, ln2_g, ln2_b, seq)
    return out.reshape(bsz, seq, d)
```
